```python
import math
import jax
import jax.numpy as jnp
from jax import lax
import numpy as np

D_MODEL = 1024
BATCH = 16
SEQ = 2048
DEPTH = 2

GRID_W = 64
CTX_LEN = 256
Q_BLOCK = 128
ROPE_THETA = 10000.0
EPS = 1e-6
N_MOD = 9

HEAD_DIM = 64
GROUP_WIDTH = D_MODEL // 4
N_HEADS_A = GROUP_WIDTH // HEAD_DIM
N_KV_A = N_HEADS_A // 2
GQA_GROUP = N_HEADS_A // N_KV_A
N_HEADS_C = GROUP_WIDTH // HEAD_DIM
DIFF_DIM = HEAD_DIM // 2
DIFF_VDIM = HEAD_DIM
GMLP_GROUPS = GROUP_WIDTH // HEAD_DIM
GMLP_GROUP_DIM = HEAD_DIM
GMLP_WIDTH = GROUP_WIDTH
CHUNK = 128
CONV_CH = GROUP_WIDTH
CONV_K = 31
D_FF = ((8 * D_MODEL // 3 + 127) // 128) * 128

IN_SPLITS = (
    N_HEADS_A * HEAD_DIM,
    N_HEADS_C * 2 * DIFF_DIM,
    N_KV_A * HEAD_DIM,
    N_KV_A * HEAD_DIM,
    N_HEADS_C * 2 * DIFF_DIM,
    N_HEADS_C * DIFF_VDIM,
    2 * GMLP_WIDTH,
    2 * CONV_CH,
)
IN_COLS = sum(IN_SPLITS)
KV_LO = IN_SPLITS[0] + IN_SPLITS[1]
KV_HI = KV_LO + sum(IN_SPLITS[2:6])

kernel_name = "hybrid_parallel_group_dit_block"


def rms_norm(x, g):
    x32 = x.astype(jnp.float32)
    y = x32 * lax.rsqrt(jnp.mean(x32 * x32, axis=-1, keepdims=True) + EPS)
    return y.astype(x.dtype) * g


def split_cols(x, sizes):
    return jnp.split(x, np.cumsum(sizes)[:-1].tolist(), axis=-1)


def adaln(cvec, w, b):
    m = jax.nn.silu(cvec) @ w + b
    return m.reshape(m.shape[:-1] + (N_MOD, 1, D_MODEL))


def modulate(h, g, mod, i):
    return rms_norm(h, g) * (1 + mod[..., i + 1, :, :]) + mod[..., i, :, :]


def swiglu(x, w_in, w_out):
    a, b = jnp.split(x @ w_in, 2, axis=-1)
    return (jax.nn.silu(a) * b) @ w_out


def axial_rope_tables(row, col, head_dim):
    quarter = head_dim // 4
    inv = ROPE_THETA ** (-jnp.arange(quarter, dtype=jnp.float32) / quarter)
    ang = jnp.concatenate([row.astype(jnp.float32)[:, None] * inv,
                           col.astype(jnp.float32)[:, None] * inv], axis=-1)
    return jnp.cos(ang), jnp.sin(ang)


def apply_rope(x, cos, sin):
    x1, x2 = jnp.split(x, 2, axis=-1)
    cos = cos.astype(x.dtype)
    sin = sin.astype(x.dtype)
    return jnp.concatenate([x1 * cos - x2 * sin, x1 * sin + x2 * cos], axis=-1)


def sweep_query_blocks(fn, q):
    n, d = q.shape[-2], q.shape[-1]
    qb = jnp.moveaxis(q.reshape(q.shape[:-2] + (n // Q_BLOCK, Q_BLOCK, d)), -3, 0)
    out = jnp.moveaxis(lax.map(fn, qb), 0, -3)
    return out.reshape(out.shape[:-3] + (n, out.shape[-1]))


def to_heads(x, n_heads, d):
    b, n, _ = x.shape
    return x.reshape(b, n, n_heads, d).transpose(0, 2, 1, 3)


def gqa_q_heads(q, g):
    b, n, _ = q.shape
    q = rms_norm(q.reshape(b, n, N_KV_A, GQA_GROUP, HEAD_DIM), g)
    return q.transpose(0, 2, 3, 1, 4)


def gqa_attend(q, k, v):
    s = jnp.einsum('bkgqd,bksd->bkgqs', q, k) * HEAD_DIM ** -0.5
    p = jax.nn.softmax(s.astype(jnp.float32), axis=-1).astype(v.dtype)
    return jnp.einsum('bkgqs,bksd->bkgqd', p, v)


def merge_gqa(o):
    b, kv, g, n, d = o.shape
    return o.transpose(0, 3, 1, 2, 4).reshape(b, n, kv * g * d)


def diff_qk_heads(x):
    b, n, _ = x.shape
    return x.reshape(b, n, N_HEADS_C, 2, DIFF_DIM).transpose(0, 2, 3, 1, 4)


def diff_lambda(lp, lam_init):
    lp = lp.astype(jnp.float32)
    return jnp.exp(jnp.sum(lp[0] * lp[1])) - jnp.exp(jnp.sum(lp[2] * lp[3])) + lam_init


def diff_attend(q, k, v, lam):
    s = jnp.einsum('bhcqd,bhcsd->bhcqs', q, k) * DIFF_DIM ** -0.5
    p = jax.nn.softmax(s.astype(jnp.float32), axis=-1)
    a = (p[:, :, 0] - lam * p[:, :, 1]).astype(v.dtype)
    return jnp.einsum('bhqs,bhsd->bhqd', a, v)


def diff_finish(o, g_sub, lam_init):
    o = rms_norm(o, g_sub) * (1.0 - lam_init)
    b, h, n, d = o.shape
    return o.transpose(0, 2, 1, 3).reshape(b, n, h * d)


def chunk_gmlp(uv, g_v, w_s, b_s):
    u, v = jnp.split(jax.nn.gelu(uv), 2, axis=-1)
    v = rms_norm(v, g_v)
    b, n, _ = v.shape
    vb = v.reshape(b, n // CHUNK, CHUNK, GMLP_GROUPS, GMLP_GROUP_DIM)
    mixed = jnp.einsum('gpq,bcqgd->bcpgd', w_s, vb) + jnp.transpose(b_s)[:, :, None]
    return u * mixed.reshape(b, n, GMLP_WIDTH)


def conformer_conv(glu_in, w_dw, b_dw, g_n):
    a, gate = jnp.split(glu_in, 2, axis=-1)
    y = a * jax.nn.sigmoid(gate)
    y = lax.conv_general_dilated(
        y, w_dw[:, None, :], window_strides=(1,), padding=[(CONV_K // 2, CONV_K // 2)],
        dimension_numbers=('NWC', 'WIO', 'NWC'), feature_group_count=CONV_CH) + b_dw
    return jax.nn.silu(rms_norm(y, g_n))


def setup_inputs(seed: int = 0) -> dict:
    key = jax.random.key(seed)
    ks = jax.random.split(key, 24)
    f32 = jnp.float32

    def nrm(k, shape, scale):
        return jax.random.normal(k, shape, f32) * scale

    return {
        "x": nrm(ks[0], (BATCH, SEQ, D_MODEL), 1.0),
        "c": nrm(ks[1], (BATCH, D_MODEL), 1.0),
        "ctx": nrm(ks[2], (BATCH, CTX_LEN, D_MODEL), 1.0),
        "c_ctx": nrm(ks[3], (D_MODEL,), 1.0),
        "w_ada": nrm(ks[4], (DEPTH, D_MODEL, N_MOD * D_MODEL), 0.5 * D_MODEL ** -0.5),
        "b_ada": nrm(ks[5], (DEPTH, N_MOD * D_MODEL), 0.02),
        "g_norm": 1.0 + nrm(ks[6], (DEPTH, 3, D_MODEL), 0.02),
        "w_ff1_in": nrm(ks[7], (DEPTH, D_MODEL, 2 * D_FF), D_MODEL ** -0.5),
        "w_ff1_out": nrm(ks[8], (DEPTH, D_FF, D_MODEL), D_FF ** -0.5),
        "w_ff2_in": nrm(ks[9], (DEPTH, D_MODEL, 2 * D_FF), D_MODEL ** -0.5),
        "w_ff2_out": nrm(ks[10], (DEPTH, D_FF, D_MODEL), D_FF ** -0.5),
        "w_in": nrm(ks[11], (DEPTH, D_MODEL, IN_COLS), D_MODEL ** -0.5),
        "w_out": nrm(ks[12], (DEPTH, D_MODEL, D_MODEL), D_MODEL ** -0.5),
        "g_q_a": 1.0 + nrm(ks[13], (DEPTH, HEAD_DIM), 0.02),
        "g_k_a": 1.0 + nrm(ks[14], (DEPTH, HEAD_DIM), 0.02),
        "lam_c": nrm(ks[15], (DEPTH, 4, DIFF_DIM), 0.1),
        "g_sub_c": 1.0 + nrm(ks[16], (DEPTH, DIFF_VDIM), 0.02),
        "g_v_b": 1.0 + nrm(ks[17], (DEPTH, GMLP_WIDTH), 0.02),
        "w_s_b": nrm(ks[18], (DEPTH, GMLP_GROUPS, CHUNK, CHUNK), CHUNK ** -0.5),
        "b_s_b": 1.0 + nrm(ks[19], (DEPTH, GMLP_GROUPS, CHUNK), 0.02),
        "w_dw_d": nrm(ks[20], (DEPTH, CONV_K, CONV_CH), CONV_K ** -0.5),
        "b_dw_d": nrm(ks[21], (DEPTH, CONV_CH), 0.02),
        "g_conv_d": 1.0 + nrm(ks[22], (DEPTH, CONV_CH), 0.02),
        "g_final": 1.0 + nrm(ks[23], (D_MODEL,), 0.02),
    }


def reference(x, c, ctx, c_ctx, w_ada, b_ada, g_norm, w_ff1_in, w_ff1_out, w_ff2_in,
              w_ff2_out, w_in, w_out, g_q_a, g_k_a, lam_c, g_sub_c, g_v_b, w_s_b, b_s_b,
              w_dw_d, b_dw_d, g_conv_d, g_final):
    n = x.shape[1]
    rows = n // GRID_W
    row = jnp.repeat(jnp.arange(rows, dtype=jnp.int32), GRID_W)
    col = jnp.tile(jnp.arange(GRID_W, dtype=jnp.int32), rows)
    cos_a, sin_a = axial_rope_tables(row, col, HEAD_DIM)
    cos_c, sin_c = axial_rope_tables(row, col, DIFF_DIM)

    h, hc = x, ctx
    for l in range(DEPTH):
        last = l == DEPTH - 1
        m = adaln(c, w_ada[l], b_ada[l])
        mc = adaln(c_ctx, w_ada[l], b_ada[l])

        h = h + 0.5 * m[..., 2, :, :] * swiglu(modulate(h, g_norm[l, 0], m, 0), w_ff1_in[l], w_ff1_out[l])
        hc = hc + 0.5 * mc[..., 2, :, :] * swiglu(modulate(hc, g_norm[l, 0], mc, 0), w_ff1_in[l], w_ff1_out[l])

        hn = modulate(h, g_norm[l, 1], m, 3)
        hcn = modulate(hc, g_norm[l, 1], mc, 3)
        qa, qc, ka, va, kc, vc, uv, glu = split_cols(hn @ w_in[l], IN_SPLITS)
        if last:
            ka_x, va_x, kc_x, vc_x = split_cols(hcn @ w_in[l, :, KV_LO:KV_HI], IN_SPLITS[2:6])
        else:
            qa_x, qc_x, ka_x, va_x, kc_x, vc_x, uv_x, glu_x = split_cols(hcn @ w_in[l], IN_SPLITS)
        ka_x = rms_norm(to_heads(ka_x, N_KV_A, HEAD_DIM), g_k_a[l])
        va_x = to_heads(va_x, N_KV_A, HEAD_DIM)
        kc_x = diff_qk_heads(kc_x)
        vc_x = to_heads(vc_x, N_HEADS_C, DIFF_VDIM)
        lam_init = 0.8 - 0.6 * math.exp(-0.3 * l)
        lam = diff_lambda(lam_c[l], lam_init)

        qa_h = apply_rope(gqa_q_heads(qa, g_q_a[l]), cos_a, sin_a)
        ka_all = jnp.concatenate(
            [apply_rope(rms_norm(to_heads(ka, N_KV_A, HEAD_DIM), g_k_a[l]), cos_a, sin_a), ka_x], axis=2)
        va_all = jnp.concatenate([to_heads(va, N_KV_A, HEAD_DIM), va_x], axis=2)
        oa = merge_gqa(sweep_query_blocks(lambda qb: gqa_attend(qb, ka_all, va_all), qa_h))

        qc_h = apply_rope(diff_qk_heads(qc), cos_c, sin_c)
        kc_all = jnp.concatenate([apply_rope(diff_qk_heads(kc), cos_c, sin_c), kc_x], axis=3)
        vc_all = jnp.concatenate([to_heads(vc, N_HEADS_C, DIFF_VDIM), vc_x], axis=2)
        oc = diff_finish(sweep_query_blocks(lambda qb: diff_attend(qb, kc_all, vc_all, lam), qc_h),
                         g_sub_c[l], lam_init)

        ob = chunk_gmlp(uv, g_v_b[l], w_s_b[l], b_s_b[l])
        od = conformer_conv(glu, w_dw_d[l], b_dw_d[l], g_conv_d[l])

        mix = jnp.concatenate([oa, oc, ob, od], axis=-1) @ w_out[l]
        h = h + m[..., 5, :, :] * mix

        if not last:
            oa_x = merge_gqa(gqa_attend(gqa_q_heads(qa_x, g_q_a[l]), ka_x, va_x))
            oc_x = diff_finish(diff_attend(diff_qk_heads(qc_x), kc_x, vc_x, lam), g_sub_c[l], lam_init)
            ob_x = chunk_gmlp(uv_x, g_v_b[l], w_s_b[l], b_s_b[l])
            od_x = conformer_conv(glu_x, w_dw_d[l], b_dw_d[l], g_conv_d[l])
            mix_x = jnp.concatenate([oa_x, oc_x, ob_x, od_x], axis=-1) @ w_out[l]
            hc = hc + mc[..., 5, :, :] * mix_x

        h = h + 0.5 * m[..., 8, :, :] * swiglu(modulate(h, g_norm[l, 2], m, 6), w_ff2_in[l], w_ff2_out[l])
        if not last:
            hc = hc + 0.5 * mc[..., 8, :, :] * swiglu(modulate(hc, g_norm[l, 2], mc, 6), w_ff2_in[l], w_ff2_out[l])

    return rms_norm(h, g_final)
```

```python
import functools
import math

import jax
import jax.numpy as jnp
import numpy as np
from jax import lax
from jax.experimental import pallas as pl
from jax.experimental.pallas import tpu as pltpu

F32 = jnp.float32
BF16 = jnp.bfloat16

D_MODEL = 1024
N_MOD = 9
HEAD_DIM = 64
DIFF_DIM = 32
GROUP_WIDTH = 256
D_FF = 2816
FF_CHUNK = 256
N_FF_CHUNKS = D_FF // FF_CHUNK
CHUNK = 128
CONV_K = 31
CONV_PAD = 16
GRID_W = 64
ROPE_THETA = 10000.0
EPS = 1e-6
IN_COLS = 2304
C_QA, C_QC, C_KA, C_VA, C_KC, C_VC, C_UV, C_GLU = 0, 256, 512, 640, 768, 1024, 1280, 1792

VMEM_LIMIT_BYTES = 56 * 1024 * 1024


def _params(n_grid_dims):
    return pltpu.CompilerParams(
        dimension_semantics=("parallel",) * n_grid_dims,
        vmem_limit_bytes=VMEM_LIMIT_BYTES)


def _resident(shape):
    zeros = (0,) * len(shape)
    return pl.BlockSpec(shape, lambda *_: zeros, pipeline_mode=pl.Buffered(1))


def _rms(x, axis=-1):
    return x * lax.rsqrt(jnp.mean(x * x, axis=axis, keepdims=True) + EPS)


def _modulated_norm(x, g, mod, base):
    shift = mod[base:base + 1, :]
    scale = mod[base + 1:base + 2, :]
    return (_rms(x) * g) * (1.0 + scale) + shift


def _ada_kernel(c_ref, w_ref, b_ref, o_ref):
    c = c_ref[...]
    s = (c * jax.nn.sigmoid(c)).astype(BF16)
    o_ref[0] = jnp.dot(s, w_ref[0].astype(BF16), preferred_element_type=F32) + b_ref[0]


def _ada(cc, w_ada, b_ada):
    depth, d, n = w_ada.shape
    rows = cc.shape[0]
    tn = 1024
    return pl.pallas_call(
        _ada_kernel,
        out_shape=jax.ShapeDtypeStruct((depth, rows, n), F32),
        grid=(depth, n // tn),
        in_specs=[
            pl.BlockSpec((rows, d), lambda l, j: (0, 0)),
            pl.BlockSpec((1, d, tn), lambda l, j: (l, 0, j)),
            pl.BlockSpec((1, 1, tn), lambda l, j: (l, 0, j)),
        ],
        out_specs=pl.BlockSpec((1, rows, tn), lambda l, j: (l, 0, j)),
        compiler_params=_params(2),
        name="adaln",
    )(cc, w_ada, b_ada.reshape(depth, 1, n))


def _ffn_kernel(*refs, mod_base, final):
    if final:
        x_ref, mod_ref, g_ref, wab_ref, wo_ref, gf_ref, o_ref, xb_ref, hm_ref = refs
    else:
        x_ref, mod_ref, g_ref, wab_ref, wo_ref, o_ref, xb_ref, hm_ref = refs
    x = x_ref[0]
    mod = mod_ref[0]
    xb_ref[...] = _modulated_norm(x, g_ref[...], mod, mod_base).astype(BF16)
    for c in range(N_FF_CHUNKS):
        ab = jnp.dot(xb_ref[...], wab_ref[c], preferred_element_type=F32)
        a = ab[:, :FF_CHUNK]
        b = ab[:, FF_CHUNK:]
        hm_ref[:, c * FF_CHUNK:(c + 1) * FF_CHUNK] = ((a * jax.nn.sigmoid(a)) * b).astype(BF16)
    y = jnp.dot(hm_ref[...], wo_ref[...], preferred_element_type=F32)
    gate = mod[mod_base + 2:mod_base + 3, :]
    out = x + (0.5 * gate) * y
    if final:
        out = _rms(out) * gf_ref[...]
    o_ref[0] = out


def _ffn(x, mod, g, wab, wo, mod_base, tm, g_final=None):
    nb, n, d = x.shape
    tm = min(tm, n)
    per_batch_mod = mod.shape[0] == nb and nb > 1
    mod_map = (lambda b, i: (b, 0, 0)) if per_batch_mod else (lambda b, i: (0, 0, 0))
    final = g_final is not None
    in_specs = [
        pl.BlockSpec((1, tm, d), lambda b, i: (b, i, 0)),
        pl.BlockSpec((1, N_MOD, d), mod_map),
        _resident((1, d)),
        _resident(wab.shape),
        _resident(wo.shape),
    ]
    args = [x, mod, g.reshape(1, d), wab, wo]
    if final:
        in_specs.append(_resident((1, d)))
        args.append(g_final.reshape(1, d))
    return pl.pallas_call(
        functools.partial(_ffn_kernel, mod_base=mod_base, final=final),
        out_shape=jax.ShapeDtypeStruct(x.shape, F32),
        grid=(nb, n // tm),
        in_specs=in_specs,
        out_specs=pl.BlockSpec((1, tm, d), lambda b, i: (b, i, 0)),
        scratch_shapes=[pltpu.VMEM((tm, d), BF16), pltpu.VMEM((tm, D_FF), BF16)],
        compiler_params=_params(2),
        name="ffn",
    )(*args)


def _swap_halves(x, seg):
    half = seg // 2
    width = x.shape[-1]
    lane = lax.broadcasted_iota(jnp.int32, x.shape, x.ndim - 1)
    first = (lane & (seg - 1)) < half
    return jnp.where(first, pltpu.roll(x, width - half, x.ndim - 1), pltpu.roll(x, half, x.ndim - 1))


def _segment_mean_sq(x, seg_ones, seg):
    sq = x * x
    hi = sq.astype(BF16)
    lo = (sq - hi.astype(F32)).astype(BF16)
    tot = (jnp.dot(hi, seg_ones, preferred_element_type=F32)
           + jnp.dot(lo, seg_ones, preferred_element_type=F32))
    return tot * (1.0 / seg)


def _gelu_tanh(x):
    cdf = 0.5 * (1.0 + jnp.tanh(math.sqrt(2.0 / math.pi) * (x + 0.044715 * (x * x * x))))
    return x * cdf


def _proj_kernel(x_ref, mod_ref, g_ref, w_ref, gq_ref, gk_ref, gv_ref, ws_ref, bs_ref,
                 cosa_ref, sina_ref, cosc_ref, sinc_ref, seg_ref,
                 qa_ref, qc_ref, ka_ref, vta_ref, kc_ref, vtc_ref, ob_ref, yc_ref, *, rope):
    x = x_ref[0]
    tm = x.shape[0]
    xb = _modulated_norm(x, g_ref[...], mod_ref[0], 3).astype(BF16)
    y = jnp.dot(xb, w_ref[...], preferred_element_type=F32)
    seg = seg_ref[...]

    def rope_a(v, width):
        if not rope:
            return v
        return v * cosa_ref[:, :width] + _swap_halves(v, HEAD_DIM) * sina_ref[:, :width]

    def rope_c(v):
        if not rope:
            return v
        return v * cosc_ref[...] + _swap_halves(v, DIFF_DIM) * sinc_ref[...]

    qa = y[:, C_QA:C_QA + 256]
    qa = qa * lax.rsqrt(_segment_mean_sq(qa, seg, HEAD_DIM) + EPS) * gq_ref[...]
    qa_ref[0] = (rope_a(qa, 256) * (HEAD_DIM ** -0.5)).astype(BF16)
    ka = y[:, C_KA:C_KA + 128]
    ka = ka * lax.rsqrt(_segment_mean_sq(ka, seg[:128, :128], HEAD_DIM) + EPS) * gk_ref[...]
    ka_ref[0] = rope_a(ka, 128).astype(BF16)
    vta_ref[0] = y[:, C_VA:C_VA + 128].T.astype(BF16)

    qc_ref[0] = (rope_c(y[:, C_QC:C_QC + 256]) * (DIFF_DIM ** -0.5)).astype(BF16)
    kc_ref[0] = rope_c(y[:, C_KC:C_KC + 256]).astype(BF16)
    vtc_ref[0] = y[:, C_VC:C_VC + 256].T.astype(BF16)

    uv = _gelu_tanh(y[:, C_UV:C_UV + 512])
    u = uv[:, :256]
    vn = (_rms(uv[:, 256:]) * gv_ref[...]).astype(BF16)
    group = lax.broadcasted_iota(jnp.int32, (CHUNK, GROUP_WIDTH), 1) // HEAD_DIM
    for j in range(tm // CHUNK):
        rows = slice(j * CHUNK, (j + 1) * CHUNK)
        r = jnp.dot(ws_ref[...], vn[rows], preferred_element_type=F32)
        mixed = r[0:CHUNK]
        for gi in range(1, 4):
            mixed = jnp.where(group == gi, r[gi * CHUNK:(gi + 1) * CHUNK], mixed)
        ob_ref[0, rows, :] = (u[rows] * (mixed + bs_ref[...])).astype(BF16)

    glu = y[:, C_GLU:C_GLU + 512]
    yc_ref[0] = glu[:, :256] * jax.nn.sigmoid(glu[:, 256:])


def _proj(x, mod, g, w_in, p, tables, tm, rope):
    nb, n, d = x.shape
    tm = min(tm, n)
    per_batch_mod = mod.shape[0] == nb and nb > 1
    mod_map = (lambda b, i: (b, 0, 0)) if per_batch_mod else (lambda b, i: (0, 0, 0))
    tok = lambda w: pl.BlockSpec((1, tm, w), lambda b, i: (b, i, 0))
    tokt = lambda w: pl.BlockSpec((1, w, tm), lambda b, i: (b, 0, i))
    tab = lambda w: pl.BlockSpec((tm, w), lambda b, i: (i, 0))
    cosa, sina, cosc, sinc = tables
    out_shape = [
        jax.ShapeDtypeStruct((nb, n, 256), BF16),
        jax.ShapeDtypeStruct((nb, n, 256), BF16),
        jax.ShapeDtypeStruct((nb, n, 128), BF16),
        jax.ShapeDtypeStruct((nb, 128, n), BF16),
        jax.ShapeDtypeStruct((nb, n, 256), BF16),
        jax.ShapeDtypeStruct((nb, 256, n), BF16),
        jax.ShapeDtypeStruct((nb, n, 256), BF16),
        jax.ShapeDtypeStruct((nb, n, 256), F32),
    ]
    return pl.pallas_call(
        functools.partial(_proj_kernel, rope=rope),
        out_shape=out_shape,
        grid=(nb, n // tm),
        in_specs=[
            tok(d),
            pl.BlockSpec((1, N_MOD, d), mod_map),
            _resident((1, d)),
            _resident(w_in.shape),
            _resident((1, 256)), _resident((1, 128)), _resident((1, 256)),
            _resident(p["ws"].shape), _resident((CHUNK, 256)),
            tab(256), tab(256), tab(256), tab(256),
            _resident((256, 256)),
        ],
        out_specs=[tok(256), tok(256), tok(128), tokt(128), tok(256), tokt(256), tok(256), tok(256)],
        compiler_params=_params(2),
        name="proj",
    )(x, mod, g.reshape(1, d), w_in, p["gq"], p["gk"], p["gv"], p["ws"], p["bs"],
      cosa, sina, cosc, sinc, p["seg"])


def _attend_t(qm, k, vt):
    st = lax.dot_general(k, qm, (((1,), (1,)), ((), ())), preferred_element_type=F32)
    m = jnp.max(st, axis=0, keepdims=True)
    p = jnp.exp(st - m)
    l = jnp.sum(p, axis=0, keepdims=True)
    ot = jnp.dot(vt, p.astype(BF16), preferred_element_type=F32)
    return ot, l


def _mix_kernel(h_ref, mod_ref, qa_ref, qc_ref, ka_ref, vta_ref, kc_ref, vtc_ref, ob_ref, yc_ref,
                wdw_ref, bdw_ref, gconv_ref, gsub_ref, lamc_ref, wout_ref, o_ref, win_ref,
                *, lam_init, tq):
    qi = pl.program_id(1)

    def masked(q, lo, width):
        lane = lax.broadcasted_iota(jnp.int32, q.shape, 1)
        keep = (lane >= lo) & (lane < lo + width)
        return jnp.where(keep, q, 0.0).astype(BF16)

    qa = qa_ref[0].astype(F32)
    ka = ka_ref[0]
    oa = [None] * 4
    for r in range(2):
        qr = qa[:, 128 * r:128 * (r + 1)]
        for g in range(2):
            ot, l = _attend_t(masked(qr, HEAD_DIM * g, HEAD_DIM), ka,
                              vta_ref[0, HEAD_DIM * g:HEAD_DIM * (g + 1), :])
            oa[2 * g + r] = ot / l
    oat = jnp.concatenate(oa, axis=0)

    lp = lamc_ref[...]
    lam = (jnp.exp(jnp.sum(lp[0:1] * lp[1:2], keepdims=True))
           - jnp.exp(jnp.sum(lp[2:3] * lp[3:4], keepdims=True)) + lam_init)
    qc = qc_ref[0].astype(F32)
    kc = kc_ref[0]
    oc = []
    for h in range(4):
        vt = vtc_ref[0, HEAD_DIM * h:HEAD_DIM * (h + 1), :]
        o0, l0 = _attend_t(masked(qc, HEAD_DIM * h, DIFF_DIM), kc, vt)
        o1, l1 = _attend_t(masked(qc, HEAD_DIM * h + DIFF_DIM, DIFF_DIM), kc, vt)
        o = o0 / l0 - lam * (o1 / l1)
        oc.append(_rms(o, axis=0) * gsub_ref[...] * (1.0 - lam_init))
    oct = jnp.concatenate(oc, axis=0)

    start = pl.multiple_of(qi * tq, 8)
    win_ref[...] = yc_ref[0, pl.ds(start, tq + 2 * CONV_PAD), :]
    acc = win_ref[pl.ds(CONV_PAD - CONV_K // 2, tq), :] * wdw_ref[0:1, :]
    for k in range(1, CONV_K):
        acc = acc + win_ref[pl.ds(CONV_PAD - CONV_K // 2 + k, tq), :] * wdw_ref[k:k + 1, :]
    od = _rms(acc + bdw_ref[...]) * gconv_ref[...]
    od = od * jax.nn.sigmoid(od)

    xcat = jnp.concatenate([oat.T.astype(BF16), oct.T.astype(BF16), ob_ref[0], od.astype(BF16)], axis=1)
    mix = jnp.dot(xcat, wout_ref[...], preferred_element_type=F32)
    o_ref[0] = h_ref[0] + mod_ref[0][5:6, :] * mix


def _mix(h, mod, qa, qc, ka, vta, kc, vtc, ob, yc_pad, p, w_out, lam_init, tq):
    nb, n, d = h.shape
    tq = min(tq, n)
    per_batch_mod = mod.shape[0] == nb and nb > 1
    mod_map = (lambda b, i: (b, 0, 0)) if per_batch_mod else (lambda b, i: (0, 0, 0))
    tok = lambda w: pl.BlockSpec((1, tq, w), lambda b, i: (b, i, 0))
    per_batch = lambda a: pl.BlockSpec((1,) + a.shape[1:], lambda b, i: (b, 0, 0))
    return pl.pallas_call(
        functools.partial(_mix_kernel, lam_init=lam_init, tq=tq),
        out_shape=jax.ShapeDtypeStruct(h.shape, F32),
        grid=(nb, n // tq),
        in_specs=[
            tok(d),
            pl.BlockSpec((1, N_MOD, d), mod_map),
            tok(256), tok(256),
            per_batch(ka), per_batch(vta), per_batch(kc), per_batch(vtc),
            tok(256),
            per_batch(yc_pad),
            _resident((CONV_K, 256)), _resident((1, 256)), _resident((1, 256)),
            _resident((HEAD_DIM, 1)), _resident((4, DIFF_DIM)),
            _resident((d, d)),
        ],
        out_specs=tok(d),
        scratch_shapes=[pltpu.VMEM((tq + 2 * CONV_PAD, 256), F32)],
        compiler_params=_params(2),
        name="mix",
    )(h, mod, qa, qc, ka, vta, kc, vtc, ob, yc_pad,
      p["wdw"], p["bdw"], p["gconv"], p["gsub"], p["lamc"], w_out)


def _rope_tables(n):
    t = np.arange(n)
    row, col = (t // GRID_W).astype(np.float32), (t % GRID_W).astype(np.float32)

    def table(head_dim):
        quarter = head_dim // 4
        inv = (ROPE_THETA ** (-jnp.arange(quarter, dtype=F32) / quarter))
        ang = jnp.concatenate([jnp.asarray(row)[:, None] * inv, jnp.asarray(col)[:, None] * inv], axis=-1)
        cos, sin = jnp.cos(ang), jnp.sin(ang)
        reps = 256 // head_dim
        return (jnp.tile(jnp.concatenate([cos, cos], axis=-1), (1, reps)),
                jnp.tile(jnp.concatenate([-sin, sin], axis=-1), (1, reps)))

    cosa, sina = table(HEAD_DIM)
    cosc, sinc = table(DIFF_DIM)
    return cosa, sina, cosc, sinc


def _layer_params(l, w_ff1_in, w_ff1_out, w_ff2_in, w_ff2_out, w_in, w_out, g_q_a, g_k_a, lam_c,
                  g_sub_c, g_v_b, w_s_b, b_s_b, w_dw_d, b_dw_d, g_conv_d):
    def ff_in(w):
        a = w[:, :D_FF].reshape(D_MODEL, N_FF_CHUNKS, FF_CHUNK)
        b = w[:, D_FF:].reshape(D_MODEL, N_FF_CHUNKS, FF_CHUNK)
        return jnp.concatenate([a, b], axis=-1).transpose(1, 0, 2).astype(BF16)

    wi = w_in[l]
    wi = jnp.concatenate([wi[:, 0:64], wi[:, 128:192], wi[:, 64:128], wi[:, 192:256], wi[:, 256:]], axis=1)
    seg = np.kron(np.eye(256 // HEAD_DIM, dtype=np.float32), np.ones((HEAD_DIM, HEAD_DIM), np.float32))
    return dict(
        ff1_ab=ff_in(w_ff1_in[l]), ff1_o=w_ff1_out[l].astype(BF16),
        ff2_ab=ff_in(w_ff2_in[l]), ff2_o=w_ff2_out[l].astype(BF16),
        w_in=wi.astype(BF16), w_out=w_out[l].astype(BF16),
        gq=jnp.tile(g_q_a[l], 4).reshape(1, 256), gk=jnp.tile(g_k_a[l], 2).reshape(1, 128),
        gv=g_v_b[l].reshape(1, 256),
        ws=w_s_b[l].reshape(4 * CHUNK, CHUNK).astype(BF16),
        bs=jnp.repeat(b_s_b[l].T, HEAD_DIM, axis=1),
        seg=jnp.asarray(seg, BF16),
        wdw=w_dw_d[l], bdw=b_dw_d[l].reshape(1, 256), gconv=g_conv_d[l].reshape(1, 256),
        gsub=g_sub_c[l].reshape(HEAD_DIM, 1), lamc=lam_c[l],
    )


def kernel(x, c, ctx, c_ctx, w_ada, b_ada, g_norm, w_ff1_in, w_ff1_out, w_ff2_in, w_ff2_out, w_in, w_out, g_q_a, g_k_a, lam_c, g_sub_c, g_v_b, w_s_b, b_s_b, w_dw_d, b_dw_d, g_conv_d, g_final):
    nb, n, d = x.shape
    n_ctx = ctx.shape[1]
    depth = w_ada.shape[0]
    tm_ffn = 512
    tm_proj = 512
    tq = 256

    rows = -(-(nb + 1) // 8) * 8
    cc = jnp.zeros((rows, d), F32).at[:nb].set(c).at[nb].set(c_ctx)
    mods = _ada(cc, w_ada, b_ada)
    tables = _rope_tables(n)
    tables_ctx = tuple(t[:n_ctx] for t in tables)

    h, hc = x, ctx
    for l in range(depth):
        last = l == depth - 1
        p = _layer_params(l, w_ff1_in, w_ff1_out, w_ff2_in, w_ff2_out, w_in, w_out, g_q_a, g_k_a,
                          lam_c, g_sub_c, g_v_b, w_s_b, b_s_b, w_dw_d, b_dw_d, g_conv_d)
        m = mods[l, :nb].reshape(nb, N_MOD, d)
        mc = mods[l, nb].reshape(1, N_MOD, d)
        lam_init = 0.8 - 0.6 * math.exp(-0.3 * l)

        h = _ffn(h, m, g_norm[l, 0], p["ff1_ab"], p["ff1_o"], 0, tm_ffn)
        hc = _ffn(hc.reshape(1, nb * n_ctx, d), mc, g_norm[l, 0], p["ff1_ab"], p["ff1_o"], 0,
                  tm_ffn).reshape(nb, n_ctx, d)

        qa, qc, ka, vta, kc, vtc, ob, yc = _proj(h, m, g_norm[l, 1], p["w_in"], p, tables, tm_proj, True)
        qa_x, qc_x, ka_x, vta_x, kc_x, vtc_x, ob_x, yc_x = _proj(
            hc, mc, g_norm[l, 1], p["w_in"], p, tables_ctx, tm_proj, False)

        pad = ((0, 0), (CONV_PAD, CONV_PAD), (0, 0))
        h = _mix(h, m, qa, qc,
                 jnp.concatenate([ka, ka_x], axis=1), jnp.concatenate([vta, vta_x], axis=2),
                 jnp.concatenate([kc, kc_x], axis=1), jnp.concatenate([vtc, vtc_x], axis=2),
                 ob, jnp.pad(yc, pad), p, p["w_out"], lam_init, tq)
        if not last:
            hc = _mix(hc, mc, qa_x, qc_x, ka_x, vta_x, kc_x, vtc_x, ob_x, jnp.pad(yc_x, pad),
                      p, p["w_out"], lam_init, tq)

        h = _ffn(h, m, g_norm[l, 2], p["ff2_ab"], p["ff2_o"], 6, tm_ffn,
                 g_final=g_final if last else None)
        if not last:
            hc = _ffn(hc.reshape(1, nb * n_ctx, d), mc, g_norm[l, 2], p["ff2_ab"], p["ff2_o"], 6,
                      tm_ffn).reshape(nb, n_ctx, d)
    return h
```

```python
import functools
import math

import jax
import jax.numpy as jnp
import numpy as np
from jax import lax
from jax.experimental import pallas as pl
from jax.experimental.pallas import tpu as pltpu

F32 = jnp.float32
BF16 = jnp.bfloat16

D_MODEL = 1024
N_MOD = 9
HEAD_DIM = 64
DIFF_DIM = 32
GROUP_WIDTH = 256
D_FF = 2816
FF_CHUNK = 256
N_FF_CHUNKS = D_FF // FF_CHUNK
CHUNK = 128
CONV_K = 31
CONV_PAD = 16
GRID_W = 64
ROPE_THETA = 10000.0
EPS = 1e-6
IN_COLS = 2304
LOG2_E = math.log2(math.e)
FOLD_ROWS = 64
C_QA, C_QC, C_KA, C_VA, C_KC, C_VC, C_UV, C_GLU = 0, 256, 512, 640, 768, 1024, 1280, 1792

VMEM_LIMIT_BYTES = 56 * 1024 * 1024


def _params(n_grid_dims):
    return pltpu.CompilerParams(
        dimension_semantics=("parallel",) * n_grid_dims,
        vmem_limit_bytes=VMEM_LIMIT_BYTES)


def _resident(shape):
    zeros = (0,) * len(shape)
    return pl.BlockSpec(shape, lambda *_: zeros, pipeline_mode=pl.Buffered(1))


def _rms(x, axis=-1):
    return x * lax.rsqrt(jnp.mean(x * x, axis=axis, keepdims=True) + EPS)


def _modulated_norm(x, g, mod, base):
    shift = mod[base:base + 1, :]
    scale = mod[base + 1:base + 2, :]
    return (_rms(x) * g) * (1.0 + scale) + shift


def _ada_kernel(c_ref, w_ref, b_ref, o_ref):
    c = c_ref[...]
    s = (c * jax.nn.sigmoid(c)).astype(BF16)
    o_ref[0] = jnp.dot(s, w_ref[0].astype(BF16), preferred_element_type=F32) + b_ref[0]


def _ada(cc, w_ada, b_ada):
    depth, d, n = w_ada.shape
    rows = cc.shape[0]
    tn = 1024
    return pl.pallas_call(
        _ada_kernel,
        out_shape=jax.ShapeDtypeStruct((depth, rows, n), F32),
        grid=(depth, n // tn),
        in_specs=[
            pl.BlockSpec((rows, d), lambda l, j: (0, 0)),
            pl.BlockSpec((1, d, tn), lambda l, j: (l, 0, j)),
            pl.BlockSpec((1, 1, tn), lambda l, j: (l, 0, j)),
        ],
        out_specs=pl.BlockSpec((1, rows, tn), lambda l, j: (l, 0, j)),
        compiler_params=_params(2),
        name="adaln",
    )(cc, w_ada, b_ada.reshape(depth, 1, n))


def _ffn_kernel(*refs, mod_base, final):
    if final:
        x_ref, mod_ref, g_ref, wab_ref, wo_ref, gf_ref, o_ref, xb_ref, hm_ref = refs
    else:
        x_ref, mod_ref, g_ref, wab_ref, wo_ref, o_ref, xb_ref, hm_ref = refs
    x = x_ref[0]
    mod = mod_ref[0]
    xb_ref[...] = _modulated_norm(x, g_ref[...], mod, mod_base).astype(BF16)
    for c in range(N_FF_CHUNKS):
        ab = jnp.dot(xb_ref[...], wab_ref[c], preferred_element_type=F32)
        a = ab[:, :FF_CHUNK]
        b = ab[:, FF_CHUNK:]
        hm_ref[:, c * FF_CHUNK:(c + 1) * FF_CHUNK] = ((a * jax.nn.sigmoid(a)) * b).astype(BF16)
    y = jnp.dot(hm_ref[...], wo_ref[...], preferred_element_type=F32)
    gate = mod[mod_base + 2:mod_base + 3, :]
    out = x + (0.5 * gate) * y
    if final:
        out = _rms(out) * gf_ref[...]
    o_ref[0] = out


def _ffn(x, mod, g, wab, wo, mod_base, tm, g_final=None):
    nb, n, d = x.shape
    tm = min(tm, n)
    per_batch_mod = mod.shape[0] == nb and nb > 1
    mod_map = (lambda b, i: (b, 0, 0)) if per_batch_mod else (lambda b, i: (0, 0, 0))
    final = g_final is not None
    in_specs = [
        pl.BlockSpec((1, tm, d), lambda b, i: (b, i, 0)),
        pl.BlockSpec((1, N_MOD, d), mod_map),
        _resident((1, d)),
        _resident(wab.shape),
        _resident(wo.shape),
    ]
    args = [x, mod, g.reshape(1, d), wab, wo]
    if final:
        in_specs.append(_resident((1, d)))
        args.append(g_final.reshape(1, d))
    return pl.pallas_call(
        functools.partial(_ffn_kernel, mod_base=mod_base, final=final),
        out_shape=jax.ShapeDtypeStruct(x.shape, F32),
        grid=(nb, n // tm),
        in_specs=in_specs,
        out_specs=pl.BlockSpec((1, tm, d), lambda b, i: (b, i, 0)),
        scratch_shapes=[pltpu.VMEM((tm, d), BF16), pltpu.VMEM((tm, D_FF), BF16)],
        compiler_params=_params(2),
        name="ffn",
    )(*args)


def _swap_halves(x, seg):
    half = seg // 2
    width = x.shape[-1]
    lane = lax.broadcasted_iota(jnp.int32, x.shape, x.ndim - 1)
    first = (lane & (seg - 1)) < half
    return jnp.where(first, pltpu.roll(x, width - half, x.ndim - 1), pltpu.roll(x, half, x.ndim - 1))


def _segment_mean_sq(x, seg_ones, seg):
    sq = x * x
    hi = sq.astype(BF16)
    lo = (sq - hi.astype(F32)).astype(BF16)
    tot = (jnp.dot(hi, seg_ones, preferred_element_type=F32)
           + jnp.dot(lo, seg_ones, preferred_element_type=F32))
    return tot * (1.0 / seg)


def _gelu_tanh(x):
    cdf = 0.5 * (1.0 + jnp.tanh(math.sqrt(2.0 / math.pi) * (x + 0.044715 * (x * x * x))))
    return x * cdf


def _proj_kernel(x_ref, mod_ref, g_ref, w_ref, gq_ref, gk_ref, gv_ref, ws_ref, bs_ref,
                 cosa_ref, sina_ref, cosc_ref, sinc_ref, seg_ref,
                 qa_ref, qc_ref, ka_ref, vta_ref, kc_ref, vtc_ref, ob_ref, yc_ref, *, rope):
    x = x_ref[0]
    tm = x.shape[0]
    xb = _modulated_norm(x, g_ref[...], mod_ref[0], 3).astype(BF16)
    y = jnp.dot(xb, w_ref[...], preferred_element_type=F32)
    seg = seg_ref[...]

    def rope_a(v, width):
        if not rope:
            return v
        return v * cosa_ref[:, :width] + _swap_halves(v, HEAD_DIM) * sina_ref[:, :width]

    def rope_c(v):
        if not rope:
            return v
        return v * cosc_ref[...] + _swap_halves(v, DIFF_DIM) * sinc_ref[...]

    qa = y[:, C_QA:C_QA + 256]
    qa = qa * lax.rsqrt(_segment_mean_sq(qa, seg, HEAD_DIM) + EPS) * gq_ref[...]
    qa_ref[0] = (rope_a(qa, 256) * (HEAD_DIM ** -0.5 * LOG2_E)).astype(BF16)
    ka = y[:, C_KA:C_KA + 128]
    ka = ka * lax.rsqrt(_segment_mean_sq(ka, seg[:128, :128], HEAD_DIM) + EPS) * gk_ref[...]
    ka_ref[0] = rope_a(ka, 128).astype(BF16)
    vta_ref[0] = y[:, C_VA:C_VA + 128].T.astype(BF16)

    qc_ref[0] = (rope_c(y[:, C_QC:C_QC + 256]) * (DIFF_DIM ** -0.5 * LOG2_E)).astype(BF16)
    kc_ref[0] = rope_c(y[:, C_KC:C_KC + 256]).astype(BF16)
    vtc_ref[0] = y[:, C_VC:C_VC + 256].T.astype(BF16)

    uv = _gelu_tanh(y[:, C_UV:C_UV + 512])
    u = uv[:, :256]
    vn = (_rms(uv[:, 256:]) * gv_ref[...]).astype(BF16)
    group = lax.broadcasted_iota(jnp.int32, (CHUNK, GROUP_WIDTH), 1) // HEAD_DIM
    for j in range(tm // CHUNK):
        rows = slice(j * CHUNK, (j + 1) * CHUNK)
        r = jnp.dot(ws_ref[...], vn[rows], preferred_element_type=F32)
        mixed = r[0:CHUNK]
        for gi in range(1, 4):
            mixed = jnp.where(group == gi, r[gi * CHUNK:(gi + 1) * CHUNK], mixed)
        ob_ref[0, rows, :] = (u[rows] * (mixed + bs_ref[...])).astype(BF16)

    glu = y[:, C_GLU:C_GLU + 512]
    yc_ref[0] = glu[:, :256] * jax.nn.sigmoid(glu[:, 256:])


def _proj(x, mod, g, w_in, p, tables, tm, rope):
    nb, n, d = x.shape
    tm = min(tm, n)
    per_batch_mod = mod.shape[0] == nb and nb > 1
    mod_map = (lambda b, i: (b, 0, 0)) if per_batch_mod else (lambda b, i: (0, 0, 0))
    tok = lambda w: pl.BlockSpec((1, tm, w), lambda b, i: (b, i, 0))
    tokt = lambda w: pl.BlockSpec((1, w, tm), lambda b, i: (b, 0, i))
    tab = lambda w: pl.BlockSpec((tm, w), lambda b, i: (i, 0))
    cosa, sina, cosc, sinc = tables
    out_shape = [
        jax.ShapeDtypeStruct((nb, n, 256), BF16),
        jax.ShapeDtypeStruct((nb, n, 256), BF16),
        jax.ShapeDtypeStruct((nb, n, 128), BF16),
        jax.ShapeDtypeStruct((nb, 128, n), BF16),
        jax.ShapeDtypeStruct((nb, n, 256), BF16),
        jax.ShapeDtypeStruct((nb, 256, n), BF16),
        jax.ShapeDtypeStruct((nb, n, 256), BF16),
        jax.ShapeDtypeStruct((nb, n, 256), F32),
    ]
    return pl.pallas_call(
        functools.partial(_proj_kernel, rope=rope),
        out_shape=out_shape,
        grid=(nb, n // tm),
        in_specs=[
            tok(d),
            pl.BlockSpec((1, N_MOD, d), mod_map),
            _resident((1, d)),
            _resident(w_in.shape),
            _resident((1, 256)), _resident((1, 128)), _resident((1, 256)),
            _resident(p["ws"].shape), _resident((CHUNK, 256)),
            tab(256), tab(256), tab(256), tab(256),
            _resident((256, 256)),
        ],
        out_specs=[tok(256), tok(256), tok(128), tokt(128), tok(256), tokt(256), tok(256), tok(256)],
        compiler_params=_params(2),
        name="proj",
    )(x, mod, g.reshape(1, d), w_in, p["gq"], p["gk"], p["gv"], p["ws"], p["bs"],
      cosa, sina, cosc, sinc, p["seg"])


def _fold_rows(x, op, reduce_fn):
    n = x.shape[0]
    blk = min(FOLD_ROWS, n)
    acc = x[0:blk]
    for i in range(1, n // blk):
        acc = op(acc, x[i * blk:(i + 1) * blk])
    while blk > 8:
        blk //= 2
        acc = op(acc[:blk], acc[blk:2 * blk])
    return reduce_fn(acc, axis=0, keepdims=True)


def _scores_t(qm, k):
    return lax.dot_general(k, qm, (((1,), (1,)), ((), ())), preferred_element_type=F32)


def _softmax_pv_t(st, vt):
    m = _fold_rows(st, jnp.maximum, jnp.max)
    p = jnp.exp2(st - m)
    l = _fold_rows(p, jnp.add, jnp.sum)
    ot = jnp.dot(vt, p.astype(BF16), preferred_element_type=F32)
    return ot, l


def _attend_all_t(units):
    outs = []
    st = _scores_t(units[0][0], units[0][1])
    for i, (_, _, vt) in enumerate(units):
        st_next = _scores_t(units[i + 1][0], units[i + 1][1]) if i + 1 < len(units) else None
        outs.append(_softmax_pv_t(st, vt))
        st = st_next
    return outs


def _mix_kernel(h_ref, mod_ref, qa_ref, qc_ref, ka_ref, vta_ref, kc_ref, vtc_ref, ob_ref, yc_ref,
                wdw_ref, bdw_ref, gconv_ref, gsub_ref, lamc_ref, wout_ref, o_ref,
                *, lam_init, tq):
    qi = pl.program_id(1)

    def masked(q, lo, width):
        lane = lax.broadcasted_iota(jnp.int32, q.shape, 1)
        keep = (lane >= lo) & (lane < lo + width)
        return jnp.where(keep, q, 0.0).astype(BF16)

    qa = qa_ref[0].astype(F32)
    qc = qc_ref[0].astype(F32)
    ka = ka_ref[0]
    kc = kc_ref[0]
    units = []
    for h in range(4):
        g, r = h // 2, h % 2
        units.append((masked(qa[:, 128 * r:128 * (r + 1)], HEAD_DIM * g, HEAD_DIM), ka,
                      vta_ref[0, HEAD_DIM * g:HEAD_DIM * (g + 1), :]))
    for h in range(4):
        vt = vtc_ref[0, HEAD_DIM * h:HEAD_DIM * (h + 1), :]
        for c in range(2):
            units.append((masked(qc, HEAD_DIM * h + DIFF_DIM * c, DIFF_DIM), kc, vt))
    outs = _attend_all_t(units)

    oat = jnp.concatenate([ot / l for ot, l in outs[:4]], axis=0)

    lp = lamc_ref[...]
    lam = (jnp.exp(jnp.sum(lp[0:1] * lp[1:2], keepdims=True))
           - jnp.exp(jnp.sum(lp[2:3] * lp[3:4], keepdims=True)) + lam_init)
    oc = []
    for h in range(4):
        (o0, l0), (o1, l1) = outs[4 + 2 * h], outs[5 + 2 * h]
        o = o0 / l0 - lam * (o1 / l1)
        oc.append(_rms(o, axis=0) * gsub_ref[...] * (1.0 - lam_init))
    oct = jnp.concatenate(oc, axis=0)

    start = pl.multiple_of(qi * tq, 8)
    win = yc_ref[0, pl.ds(start, tq + 2 * CONV_PAD), :]
    n_win = tq + 2 * CONV_PAD
    acc = None
    for b in range(8):
        shifted = win if b == 0 else pltpu.roll(win, n_win - b, 0)
        for k in range(CONV_K):
            off = CONV_PAD - CONV_K // 2 + k
            if off % 8 != b:
                continue
            term = shifted[off - b:off - b + tq] * wdw_ref[k:k + 1, :]
            acc = term if acc is None else acc + term
    od = _rms(acc + bdw_ref[...]) * gconv_ref[...]
    od = od * jax.nn.sigmoid(od)

    xcat = jnp.concatenate([oat.T.astype(BF16), oct.T.astype(BF16), ob_ref[0], od.astype(BF16)], axis=1)
    mix = jnp.dot(xcat, wout_ref[...], preferred_element_type=F32)
    o_ref[0] = h_ref[0] + mod_ref[0][5:6, :] * mix


def _mix(h, mod, qa, qc, ka, vta, kc, vtc, ob, yc_pad, p, w_out, lam_init, tq):
    nb, n, d = h.shape
    tq = min(tq, n)
    per_batch_mod = mod.shape[0] == nb and nb > 1
    mod_map = (lambda b, i: (b, 0, 0)) if per_batch_mod else (lambda b, i: (0, 0, 0))
    tok = lambda w: pl.BlockSpec((1, tq, w), lambda b, i: (b, i, 0))
    per_batch = lambda a: pl.BlockSpec((1,) + a.shape[1:], lambda b, i: (b, 0, 0))
    return pl.pallas_call(
        functools.partial(_mix_kernel, lam_init=lam_init, tq=tq),
        out_shape=jax.ShapeDtypeStruct(h.shape, F32),
        grid=(nb, n // tq),
        in_specs=[
            tok(d),
            pl.BlockSpec((1, N_MOD, d), mod_map),
            tok(256), tok(256),
            per_batch(ka), per_batch(vta), per_batch(kc), per_batch(vtc),
            tok(256),
            per_batch(yc_pad),
            _resident((CONV_K, 256)), _resident((1, 256)), _resident((1, 256)),
            _resident((HEAD_DIM, 1)), _resident((4, DIFF_DIM)),
            _resident((d, d)),
        ],
        out_specs=tok(d),
        compiler_params=_params(2),
        name="mix",
    )(h, mod, qa, qc, ka, vta, kc, vtc, ob, yc_pad,
      p["wdw"], p["bdw"], p["gconv"], p["gsub"], p["lamc"], w_out)


def _rope_tables(n):
    t = np.arange(n)
    row, col = (t // GRID_W).astype(np.float32), (t % GRID_W).astype(np.float32)

    def table(head_dim):
        quarter = head_dim // 4
        inv = (ROPE_THETA ** (-jnp.arange(quarter, dtype=F32) / quarter))
        ang = jnp.concatenate([jnp.asarray(row)[:, None] * inv, jnp.asarray(col)[:, None] * inv], axis=-1)
        cos, sin = jnp.cos(ang), jnp.sin(ang)
        reps = 256 // head_dim
        return (jnp.tile(jnp.concatenate([cos, cos], axis=-1), (1, reps)),
                jnp.tile(jnp.concatenate([-sin, sin], axis=-1), (1, reps)))

    cosa, sina = table(HEAD_DIM)
    cosc, sinc = table(DIFF_DIM)
    return cosa, sina, cosc, sinc


def _layer_params(l, w_ff1_in, w_ff1_out, w_ff2_in, w_ff2_out, w_in, w_out, g_q_a, g_k_a, lam_c,
                  g_sub_c, g_v_b, w_s_b, b_s_b, w_dw_d, b_dw_d, g_conv_d):
    def ff_in(w):
        a = w[:, :D_FF].reshape(D_MODEL, N_FF_CHUNKS, FF_CHUNK)
        b = w[:, D_FF:].reshape(D_MODEL, N_FF_CHUNKS, FF_CHUNK)
        return jnp.concatenate([a, b], axis=-1).transpose(1, 0, 2).astype(BF16)

    wi = w_in[l]
    wi = jnp.concatenate([wi[:, 0:64], wi[:, 128:192], wi[:, 64:128], wi[:, 192:256], wi[:, 256:]], axis=1)
    seg = np.kron(np.eye(256 // HEAD_DIM, dtype=np.float32), np.ones((HEAD_DIM, HEAD_DIM), np.float32))
    return dict(
        ff1_ab=ff_in(w_ff1_in[l]), ff1_o=w_ff1_out[l].astype(BF16),
        ff2_ab=ff_in(w_ff2_in[l]), ff2_o=w_ff2_out[l].astype(BF16),
        w_in=wi.astype(BF16), w_out=w_out[l].astype(BF16),
        gq=jnp.tile(g_q_a[l], 4).reshape(1, 256), gk=jnp.tile(g_k_a[l], 2).reshape(1, 128),
        gv=g_v_b[l].reshape(1, 256),
        ws=w_s_b[l].reshape(4 * CHUNK, CHUNK).astype(BF16),
        bs=jnp.repeat(b_s_b[l].T, HEAD_DIM, axis=1),
        seg=jnp.asarray(seg, BF16),
        wdw=w_dw_d[l], bdw=b_dw_d[l].reshape(1, 256), gconv=g_conv_d[l].reshape(1, 256),
        gsub=g_sub_c[l].reshape(HEAD_DIM, 1), lamc=lam_c[l],
    )


def kernel(x, c, ctx, c_ctx, w_ada, b_ada, g_norm, w_ff1_in, w_ff1_out, w_ff2_in, w_ff2_out, w_in, w_out, g_q_a, g_k_a, lam_c, g_sub_c, g_v_b, w_s_b, b_s_b, w_dw_d, b_dw_d, g_conv_d, g_final):
    nb, n, d = x.shape
    n_ctx = ctx.shape[1]
    depth = w_ada.shape[0]
    tm_ffn = 512
    tm_proj = 512
    tq = 512

    rows = -(-(nb + 1) // 8) * 8
    cc = jnp.zeros((rows, d), F32).at[:nb].set(c).at[nb].set(c_ctx)
    mods = _ada(cc, w_ada, b_ada)
    tables = _rope_tables(n)
    tables_ctx = tuple(t[:n_ctx] for t in tables)

    h, hc = x, ctx
    for l in range(depth):
        last = l == depth - 1
        p = _layer_params(l, w_ff1_in, w_ff1_out, w_ff2_in, w_ff2_out, w_in, w_out, g_q_a, g_k_a,
                          lam_c, g_sub_c, g_v_b, w_s_b, b_s_b, w_dw_d, b_dw_d, g_conv_d)
        m = mods[l, :nb].reshape(nb, N_MOD, d)
        mc = mods[l, nb].reshape(1, N_MOD, d)
        lam_init = 0.8 - 0.6 * math.exp(-0.3 * l)

        h = _ffn(h, m, g_norm[l, 0], p["ff1_ab"], p["ff1_o"], 0, tm_ffn)
        hc = _ffn(hc.reshape(1, nb * n_ctx, d), mc, g_norm[l, 0], p["ff1_ab"], p["ff1_o"], 0,
                  tm_ffn).reshape(nb, n_ctx, d)

        qa, qc, ka, vta, kc, vtc, ob, yc = _proj(h, m, g_norm[l, 1], p["w_in"], p, tables, tm_proj, True)
        qa_x, qc_x, ka_x, vta_x, kc_x, vtc_x, ob_x, yc_x = _proj(
            hc, mc, g_norm[l, 1], p["w_in"], p, tables_ctx, tm_proj, False)

        pad = ((0, 0), (CONV_PAD, CONV_PAD), (0, 0))
        h = _mix(h, m, qa, qc,
                 jnp.concatenate([ka, ka_x], axis=1), jnp.concatenate([vta, vta_x], axis=2),
                 jnp.concatenate([kc, kc_x], axis=1), jnp.concatenate([vtc, vtc_x], axis=2),
                 ob, jnp.pad(yc, pad), p, p["w_out"], lam_init, tq)
        if not last:
            hc = _mix(hc, mc, qa_x, qc_x, ka_x, vta_x, kc_x, vtc_x, ob_x, jnp.pad(yc_x, pad),
                      p, p["w_out"], lam_init, tq)

        h = _ffn(h, m, g_norm[l, 2], p["ff2_ab"], p["ff2_o"], 6, tm_ffn,
                 g_final=g_final if last else None)
        if not last:
            hc = _ffn(hc.reshape(1, nb * n_ctx, d), mc, g_norm[l, 2], p["ff2_ab"], p["ff2_o"], 6,
                      tm_ffn).reshape(nb, n_ctx, d)
    return h
```

```python
import functools
import math

import jax
import jax.numpy as jnp
import numpy as np
from jax import lax
from jax.experimental import pallas as pl
from jax.experimental.pallas import tpu as pltpu

F32 = jnp.float32
BF16 = jnp.bfloat16

D_MODEL = 1024
N_MOD = 9
HEAD_DIM = 64
DIFF_DIM = 32
GROUP_WIDTH = 256
D_FF = 2816
FF_CHUNK = 256
N_FF_CHUNKS = D_FF // FF_CHUNK
CHUNK = 128
CONV_K = 31
CONV_PAD = 16
GRID_W = 64
ROPE_THETA = 10000.0
EPS = 1e-6
IN_COLS = 2304
LOG2_E = math.log2(math.e)
FOLD_ROWS = 64
ONES_ROWS = 16
C_QA, C_QC, C_KA, C_VA, C_KC, C_VC, C_UV, C_GLU = 0, 256, 512, 640, 768, 1024, 1280, 1792

VMEM_LIMIT_BYTES = 56 * 1024 * 1024


def _params(n_grid_dims, flags=None):
    return pltpu.CompilerParams(
        dimension_semantics=("parallel",) * n_grid_dims,
        vmem_limit_bytes=VMEM_LIMIT_BYTES, flags=flags)


def _resident(shape):
    zeros = (0,) * len(shape)
    return pl.BlockSpec(shape, lambda *_: zeros, pipeline_mode=pl.Buffered(1))


def _rms(x, axis=-1):
    return x * lax.rsqrt(jnp.mean(x * x, axis=axis, keepdims=True) + EPS)


def _modulated_norm(x, g, mod, base):
    shift = mod[base:base + 1, :]
    scale = mod[base + 1:base + 2, :]
    return (_rms(x) * g) * (1.0 + scale) + shift


def _ada_kernel(c_ref, w_ref, b_ref, o_ref):
    c = c_ref[...]
    s = (c * jax.nn.sigmoid(c)).astype(BF16)
    o_ref[0] = jnp.dot(s, w_ref[0].astype(BF16), preferred_element_type=F32) + b_ref[0]


def _ada(cc, w_ada, b_ada):
    depth, d, n = w_ada.shape
    rows = cc.shape[0]
    tn = 1024
    return pl.pallas_call(
        _ada_kernel,
        out_shape=jax.ShapeDtypeStruct((depth, rows, n), F32),
        grid=(depth, n // tn),
        in_specs=[
            pl.BlockSpec((rows, d), lambda l, j: (0, 0)),
            pl.BlockSpec((1, d, tn), lambda l, j: (l, 0, j)),
            pl.BlockSpec((1, 1, tn), lambda l, j: (l, 0, j)),
        ],
        out_specs=pl.BlockSpec((1, rows, tn), lambda l, j: (l, 0, j)),
        compiler_params=_params(2),
        name="adaln",
    )(cc, w_ada, b_ada.reshape(depth, 1, n))


def _ffn_kernel(*refs, mod_base, final):
    if final:
        x_ref, mod_ref, g_ref, wab_ref, wo_ref, gf_ref, o_ref, xb_ref, hm_ref = refs
    else:
        x_ref, mod_ref, g_ref, wab_ref, wo_ref, o_ref, xb_ref, hm_ref = refs
    x = x_ref[0]
    mod = mod_ref[0]
    xb_ref[...] = _modulated_norm(x, g_ref[...], mod, mod_base).astype(BF16)
    for c in range(N_FF_CHUNKS):
        ab = jnp.dot(xb_ref[...], wab_ref[c], preferred_element_type=F32)
        a = ab[:, :FF_CHUNK]
        b = ab[:, FF_CHUNK:]
        hm_ref[:, c * FF_CHUNK:(c + 1) * FF_CHUNK] = ((a * jax.nn.sigmoid(a)) * b).astype(BF16)
    y = jnp.dot(hm_ref[...], wo_ref[...], preferred_element_type=F32)
    gate = mod[mod_base + 2:mod_base + 3, :]
    out = x + (0.5 * gate) * y
    if final:
        out = _rms(out) * gf_ref[...]
    o_ref[0] = out


def _ffn(x, mod, g, wab, wo, mod_base, tm, g_final=None):
    nb, n, d = x.shape
    tm = min(tm, n)
    per_batch_mod = mod.shape[0] == nb and nb > 1
    mod_map = (lambda b, i: (b, 0, 0)) if per_batch_mod else (lambda b, i: (0, 0, 0))
    final = g_final is not None
    in_specs = [
        pl.BlockSpec((1, tm, d), lambda b, i: (b, i, 0)),
        pl.BlockSpec((1, N_MOD, d), mod_map),
        _resident((1, d)),
        _resident(wab.shape),
        _resident(wo.shape),
    ]
    args = [x, mod, g.reshape(1, d), wab, wo]
    if final:
        in_specs.append(_resident((1, d)))
        args.append(g_final.reshape(1, d))
    return pl.pallas_call(
        functools.partial(_ffn_kernel, mod_base=mod_base, final=final),
        out_shape=jax.ShapeDtypeStruct(x.shape, F32),
        grid=(nb, n // tm),
        in_specs=in_specs,
        out_specs=pl.BlockSpec((1, tm, d), lambda b, i: (b, i, 0)),
        scratch_shapes=[pltpu.VMEM((tm, d), BF16), pltpu.VMEM((tm, D_FF), BF16)],
        compiler_params=_params(2),
        name="ffn",
    )(*args)


def _swap_halves(x, seg):
    half = seg // 2
    width = x.shape[-1]
    lane = lax.broadcasted_iota(jnp.int32, x.shape, x.ndim - 1)
    first = (lane & (seg - 1)) < half
    return jnp.where(first, pltpu.roll(x, width - half, x.ndim - 1), pltpu.roll(x, half, x.ndim - 1))


def _segment_mean_sq(x, seg_ones, seg):
    sq = x * x
    hi = sq.astype(BF16)
    lo = (sq - hi.astype(F32)).astype(BF16)
    tot = (jnp.dot(hi, seg_ones, preferred_element_type=F32)
           + jnp.dot(lo, seg_ones, preferred_element_type=F32))
    return tot * (1.0 / seg)


def _gelu_tanh(x):
    cdf = 0.5 * (1.0 + jnp.tanh(math.sqrt(2.0 / math.pi) * (x + 0.044715 * (x * x * x))))
    return x * cdf


def _proj_kernel(x_ref, mod_ref, g_ref, w_ref, gq_ref, gk_ref, gv_ref, ws_ref, bs_ref,
                 cosa_ref, sina_ref, cosc_ref, sinc_ref, seg_ref,
                 qa_ref, qc_ref, ka_ref, vta_ref, kc_ref, vtc_ref, ob_ref, yc_ref, *, rope):
    x = x_ref[0]
    tm = x.shape[0]
    xb = _modulated_norm(x, g_ref[...], mod_ref[0], 3).astype(BF16)
    y = jnp.dot(xb, w_ref[...], preferred_element_type=F32)
    seg = seg_ref[...]

    def rope_a(v, width):
        if not rope:
            return v
        return v * cosa_ref[:, :width] + _swap_halves(v, HEAD_DIM) * sina_ref[:, :width]

    def rope_c(v):
        if not rope:
            return v
        return v * cosc_ref[...] + _swap_halves(v, DIFF_DIM) * sinc_ref[...]

    qa = y[:, C_QA:C_QA + 256]
    qa = qa * lax.rsqrt(_segment_mean_sq(qa, seg, HEAD_DIM) + EPS) * gq_ref[...]
    qa_ref[0] = (rope_a(qa, 256) * (HEAD_DIM ** -0.5 * LOG2_E)).astype(BF16)
    ka = y[:, C_KA:C_KA + 128]
    ka = ka * lax.rsqrt(_segment_mean_sq(ka, seg[:128, :128], HEAD_DIM) + EPS) * gk_ref[...]
    ka_ref[0] = rope_a(ka, 128).astype(BF16)
    vta_ref[0] = y[:, C_VA:C_VA + 128].T.astype(BF16)

    qc_ref[0] = (rope_c(y[:, C_QC:C_QC + 256]) * (DIFF_DIM ** -0.5 * LOG2_E)).astype(BF16)
    kc_ref[0] = rope_c(y[:, C_KC:C_KC + 256]).astype(BF16)
    vtc_ref[0] = y[:, C_VC:C_VC + 256].T.astype(BF16)

    uv = _gelu_tanh(y[:, C_UV:C_UV + 512])
    u = uv[:, :256]
    vn = (_rms(uv[:, 256:]) * gv_ref[...]).astype(BF16)
    group = lax.broadcasted_iota(jnp.int32, (CHUNK, GROUP_WIDTH), 1) // HEAD_DIM
    for j in range(tm // CHUNK):
        rows = slice(j * CHUNK, (j + 1) * CHUNK)
        r = jnp.dot(ws_ref[...], vn[rows], preferred_element_type=F32)
        mixed = r[0:CHUNK]
        for gi in range(1, 4):
            mixed = jnp.where(group == gi, r[gi * CHUNK:(gi + 1) * CHUNK], mixed)
        ob_ref[0, rows, :] = (u[rows] * (mixed + bs_ref[...])).astype(BF16)

    glu = y[:, C_GLU:C_GLU + 512]
    yc_ref[0] = glu[:, :256] * jax.nn.sigmoid(glu[:, 256:])


def _proj(x, mod, g, w_in, p, tables, tm, rope):
    nb, n, d = x.shape
    tm = min(tm, n)
    per_batch_mod = mod.shape[0] == nb and nb > 1
    mod_map = (lambda b, i: (b, 0, 0)) if per_batch_mod else (lambda b, i: (0, 0, 0))
    tok = lambda w: pl.BlockSpec((1, tm, w), lambda b, i: (b, i, 0))
    tokt = lambda w: pl.BlockSpec((1, w, tm), lambda b, i: (b, 0, i))
    tab = lambda w: pl.BlockSpec((tm, w), lambda b, i: (i, 0))
    cosa, sina, cosc, sinc = tables
    out_shape = [
        jax.ShapeDtypeStruct((nb, n, 256), BF16),
        jax.ShapeDtypeStruct((nb, n, 256), BF16),
        jax.ShapeDtypeStruct((nb, n, 128), BF16),
        jax.ShapeDtypeStruct((nb, 128, n), BF16),
        jax.ShapeDtypeStruct((nb, n, 256), BF16),
        jax.ShapeDtypeStruct((nb, 256, n), BF16),
        jax.ShapeDtypeStruct((nb, n, 256), BF16),
        jax.ShapeDtypeStruct((nb, n, 256), F32),
    ]
    return pl.pallas_call(
        functools.partial(_proj_kernel, rope=rope),
        out_shape=out_shape,
        grid=(nb, n // tm),
        in_specs=[
            tok(d),
            pl.BlockSpec((1, N_MOD, d), mod_map),
            _resident((1, d)),
            _resident(w_in.shape),
            _resident((1, 256)), _resident((1, 128)), _resident((1, 256)),
            _resident(p["ws"].shape), _resident((CHUNK, 256)),
            tab(256), tab(256), tab(256), tab(256),
            _resident((256, 256)),
        ],
        out_specs=[tok(256), tok(256), tok(128), tokt(128), tok(256), tokt(256), tok(256), tok(256)],
        compiler_params=_params(2),
        name="proj",
    )(x, mod, g.reshape(1, d), w_in, p["gq"], p["gk"], p["gv"], p["ws"], p["bs"],
      cosa, sina, cosc, sinc, p["seg"])


def _fold_rows(x, op, reduce_fn):
    n = x.shape[0]
    blk = min(FOLD_ROWS, n)
    acc = x[0:blk]
    for i in range(1, n // blk):
        acc = op(acc, x[i * blk:(i + 1) * blk])
    while blk > 8:
        blk //= 2
        acc = op(acc[:blk], acc[blk:2 * blk])
    return reduce_fn(acc, axis=0, keepdims=True)


def _scores_t(qm, ks):
    nt = (((1,), (1,)), ((), ()))
    return jnp.concatenate([lax.dot_general(k, qm, nt, preferred_element_type=F32) for k in ks], axis=0)


def _softmax_pv_t(st, vts):
    m = _fold_rows(st, jnp.maximum, jnp.max)
    p = jnp.exp2(st - m).astype(BF16)
    ot, row = None, 0
    for vt in vts:
        part = jnp.dot(vt, p[row:row + vt.shape[1]], preferred_element_type=F32)
        ot = part if ot is None else ot + part
        row += vt.shape[1]
    dv = ot.shape[0] - ONES_ROWS
    return ot[:dv], ot[dv:dv + 1]


def _attend_all_t(units):
    outs = []
    st = _scores_t(units[0][0], units[0][1])
    for i, (_, _, vts) in enumerate(units):
        st_next = _scores_t(units[i + 1][0], units[i + 1][1]) if i + 1 < len(units) else None
        outs.append(_softmax_pv_t(st, vts))
        st = st_next
    return outs


def _mix_kernel(*refs, lam_init, tq, n_kv):
    h_ref, mod_ref, qa_ref, qc_ref = refs[:4]
    kv_refs = [refs[4 + 4 * i:8 + 4 * i] for i in range(n_kv)]
    (ob_ref, yc_ref, wdw_ref, bdw_ref, gconv_ref, gsub_ref, lamc_ref, wout_ref,
     o_ref) = refs[4 + 4 * n_kv:]
    qi = pl.program_id(1)
    n_q = pl.num_programs(1)

    def masked(q, lo, width):
        lane = lax.broadcasted_iota(jnp.int32, q.shape, 1)
        keep = (lane >= lo) & (lane < lo + width)
        return jnp.where(keep, q, 0.0).astype(BF16)

    def with_ones(vt):
        return jnp.concatenate([vt, jnp.ones((ONES_ROWS, vt.shape[1]), BF16)], axis=0)

    qa = qa_ref[0].astype(F32)
    qc = qc_ref[0].astype(F32)
    kas = [kv[0][0] for kv in kv_refs]
    kcs = [kv[2][0] for kv in kv_refs]
    units = []
    for h in range(4):
        g, r = h // 2, h % 2
        vts = [with_ones(kv[1][0, HEAD_DIM * g:HEAD_DIM * (g + 1), :]) for kv in kv_refs]
        units.append((masked(qa[:, 128 * r:128 * (r + 1)], HEAD_DIM * g, HEAD_DIM), kas, vts))
    for h in range(4):
        vts = [with_ones(kv[3][0, HEAD_DIM * h:HEAD_DIM * (h + 1), :]) for kv in kv_refs]
        for c in range(2):
            units.append((masked(qc, HEAD_DIM * h + DIFF_DIM * c, DIFF_DIM), kcs, vts))
    outs = _attend_all_t(units)

    oat = jnp.concatenate([ot / l for ot, l in outs[:4]], axis=0)

    lp = lamc_ref[...]
    lam = (jnp.exp(jnp.sum(lp[0:1] * lp[1:2], keepdims=True))
           - jnp.exp(jnp.sum(lp[2:3] * lp[3:4], keepdims=True)) + lam_init)
    oc = []
    for h in range(4):
        (o0, l0), (o1, l1) = outs[4 + 2 * h], outs[5 + 2 * h]
        o = o0 / l0 - lam * (o1 / l1)
        oc.append(_rms(o, axis=0) * gsub_ref[...] * (1.0 - lam_init))
    oct = jnp.concatenate(oc, axis=0)

    n = yc_ref.shape[1]
    start = pl.multiple_of(qi * tq, 8)
    above = pl.multiple_of(jnp.maximum(start - CONV_PAD, 0), 8)
    below = pl.multiple_of(jnp.minimum(start + tq, n - CONV_PAD), 8)
    win = jnp.concatenate([
        jnp.where(qi > 0, yc_ref[0, pl.ds(above, CONV_PAD), :], 0.0),
        yc_ref[0, pl.ds(start, tq), :],
        jnp.where(qi < n_q - 1, yc_ref[0, pl.ds(below, CONV_PAD), :], 0.0)], axis=0)
    n_win = tq + 2 * CONV_PAD
    acc = None
    for b in range(8):
        shifted = win if b == 0 else pltpu.roll(win, n_win - b, 0)
        for k in range(CONV_K):
            off = CONV_PAD - CONV_K // 2 + k
            if off % 8 != b:
                continue
            term = shifted[off - b:off - b + tq] * wdw_ref[k:k + 1, :]
            acc = term if acc is None else acc + term
    od = _rms(acc + bdw_ref[...]) * gconv_ref[...]
    od = od * jax.nn.sigmoid(od)

    xcat = jnp.concatenate([oat.T.astype(BF16), oct.T.astype(BF16), ob_ref[0], od.astype(BF16)], axis=1)
    mix = jnp.dot(xcat, wout_ref[...], preferred_element_type=F32)
    o_ref[0] = h_ref[0] + mod_ref[0][5:6, :] * mix


def _mix(h, mod, qa, qc, kv_sets, ob, yc, p, w_out, lam_init, tq):
    nb, n, d = h.shape
    tq = min(tq, n)
    per_batch_mod = mod.shape[0] == nb and nb > 1
    mod_map = (lambda b, i: (b, 0, 0)) if per_batch_mod else (lambda b, i: (0, 0, 0))
    tok = lambda w: pl.BlockSpec((1, tq, w), lambda b, i: (b, i, 0))
    per_batch = lambda a: pl.BlockSpec((1,) + a.shape[1:], lambda b, i: (b, 0, 0))
    kv_flat = [a for kv in kv_sets for a in kv]
    return pl.pallas_call(
        functools.partial(_mix_kernel, lam_init=lam_init, tq=tq, n_kv=len(kv_sets)),
        out_shape=jax.ShapeDtypeStruct(h.shape, F32),
        grid=(nb, n // tq),
        in_specs=[
            tok(d),
            pl.BlockSpec((1, N_MOD, d), mod_map),
            tok(256), tok(256),
            *[per_batch(a) for a in kv_flat],
            tok(256),
            per_batch(yc),
            _resident((CONV_K, 256)), _resident((1, 256)), _resident((1, 256)),
            _resident((HEAD_DIM, 1)), _resident((4, DIFF_DIM)),
            _resident((d, d)),
        ],
        out_specs=tok(d),
        compiler_params=_params(2),
        name="mix",
    )(h, mod, qa, qc, *kv_flat, ob, yc,
      p["wdw"], p["bdw"], p["gconv"], p["gsub"], p["lamc"], w_out)


def _rope_tables(n):
    t = np.arange(n)
    row, col = (t // GRID_W).astype(np.float32), (t % GRID_W).astype(np.float32)

    def table(head_dim):
        quarter = head_dim // 4
        inv = (ROPE_THETA ** (-jnp.arange(quarter, dtype=F32) / quarter))
        ang = jnp.concatenate([jnp.asarray(row)[:, None] * inv, jnp.asarray(col)[:, None] * inv], axis=-1)
        cos, sin = jnp.cos(ang), jnp.sin(ang)
        reps = 256 // head_dim
        return (jnp.tile(jnp.concatenate([cos, cos], axis=-1), (1, reps)),
                jnp.tile(jnp.concatenate([-sin, sin], axis=-1), (1, reps)))

    cosa, sina = table(HEAD_DIM)
    cosc, sinc = table(DIFF_DIM)
    return cosa, sina, cosc, sinc


def _layer_params(l, w_ff1_in, w_ff1_out, w_ff2_in, w_ff2_out, w_in, w_out, g_q_a, g_k_a, lam_c,
                  g_sub_c, g_v_b, w_s_b, b_s_b, w_dw_d, b_dw_d, g_conv_d):
    def ff_in(w):
        a = w[:, :D_FF].reshape(D_MODEL, N_FF_CHUNKS, FF_CHUNK)
        b = w[:, D_FF:].reshape(D_MODEL, N_FF_CHUNKS, FF_CHUNK)
        return jnp.concatenate([a, b], axis=-1).transpose(1, 0, 2).astype(BF16)

    wi = w_in[l]
    wi = jnp.concatenate([wi[:, 0:64], wi[:, 128:192], wi[:, 64:128], wi[:, 192:256], wi[:, 256:]], axis=1)
    seg = np.kron(np.eye(256 // HEAD_DIM, dtype=np.float32), np.ones((HEAD_DIM, HEAD_DIM), np.float32))
    return dict(
        ff1_ab=ff_in(w_ff1_in[l]), ff1_o=w_ff1_out[l].astype(BF16),
        ff2_ab=ff_in(w_ff2_in[l]), ff2_o=w_ff2_out[l].astype(BF16),
        w_in=wi.astype(BF16), w_out=w_out[l].astype(BF16),
        gq=jnp.tile(g_q_a[l], 4).reshape(1, 256), gk=jnp.tile(g_k_a[l], 2).reshape(1, 128),
        gv=g_v_b[l].reshape(1, 256),
        ws=w_s_b[l].reshape(4 * CHUNK, CHUNK).astype(BF16),
        bs=jnp.repeat(b_s_b[l].T, HEAD_DIM, axis=1),
        seg=jnp.asarray(seg, BF16),
        wdw=w_dw_d[l], bdw=b_dw_d[l].reshape(1, 256), gconv=g_conv_d[l].reshape(1, 256),
        gsub=g_sub_c[l].reshape(HEAD_DIM, 1), lamc=lam_c[l],
    )


def kernel(x, c, ctx, c_ctx, w_ada, b_ada, g_norm, w_ff1_in, w_ff1_out, w_ff2_in, w_ff2_out, w_in, w_out, g_q_a, g_k_a, lam_c, g_sub_c, g_v_b, w_s_b, b_s_b, w_dw_d, b_dw_d, g_conv_d, g_final):
    nb, n, d = x.shape
    n_ctx = ctx.shape[1]
    depth = w_ada.shape[0]
    tm_ffn = 512
    tm_proj = 512
    tq = 512

    rows = -(-(nb + 1) // 8) * 8
    cc = jnp.zeros((rows, d), F32).at[:nb].set(c).at[nb].set(c_ctx)
    mods = _ada(cc, w_ada, b_ada)
    tables = _rope_tables(n)
    tables_ctx = tuple(t[:n_ctx] for t in tables)

    h, hc = x, ctx
    for l in range(depth):
        last = l == depth - 1
        p = _layer_params(l, w_ff1_in, w_ff1_out, w_ff2_in, w_ff2_out, w_in, w_out, g_q_a, g_k_a,
                          lam_c, g_sub_c, g_v_b, w_s_b, b_s_b, w_dw_d, b_dw_d, g_conv_d)
        m = mods[l, :nb].reshape(nb, N_MOD, d)
        mc = mods[l, nb].reshape(1, N_MOD, d)
        lam_init = 0.8 - 0.6 * math.exp(-0.3 * l)

        h = _ffn(h, m, g_norm[l, 0], p["ff1_ab"], p["ff1_o"], 0, tm_ffn)
        hc = _ffn(hc.reshape(1, nb * n_ctx, d), mc, g_norm[l, 0], p["ff1_ab"], p["ff1_o"], 0,
                  tm_ffn).reshape(nb, n_ctx, d)

        qa, qc, ka, vta, kc, vtc, ob, yc = _proj(h, m, g_norm[l, 1], p["w_in"], p, tables, tm_proj, True)
        qa_x, qc_x, ka_x, vta_x, kc_x, vtc_x, ob_x, yc_x = _proj(
            hc, mc, g_norm[l, 1], p["w_in"], p, tables_ctx, tm_proj, False)

        kv_x = (ka_x, vta_x, kc_x, vtc_x)
        h = _mix(h, m, qa, qc, [(ka, vta, kc, vtc), kv_x], ob, yc, p, p["w_out"], lam_init, tq)
        if not last:
            hc = _mix(hc, mc, qa_x, qc_x, [kv_x], ob_x, yc_x, p, p["w_out"], lam_init, tq)

        h = _ffn(h, m, g_norm[l, 2], p["ff2_ab"], p["ff2_o"], 6, tm_ffn,
                 g_final=g_final if last else None)
        if not last:
            hc = _ffn(hc.reshape(1, nb * n_ctx, d), mc, g_norm[l, 2], p["ff2_ab"], p["ff2_o"], 6,
                      tm_ffn).reshape(nb, n_ctx, d)
    return h
```

```python
import functools
import math

import jax
import jax.numpy as jnp
import numpy as np
from jax import lax
from jax.experimental import pallas as pl
from jax.experimental.pallas import tpu as pltpu

F32 = jnp.float32
BF16 = jnp.bfloat16

D_MODEL = 1024
N_MOD = 9
HEAD_DIM = 64
DIFF_DIM = 32
GROUP_WIDTH = 256
D_FF = 2816
FF_CHUNK = 256
N_FF_CHUNKS = D_FF // FF_CHUNK
FFN_ROW_SPLIT = 2
CHUNK = 128
CONV_K = 31
CONV_PAD = 16
GRID_W = 64
ROPE_THETA = 10000.0
EPS = 1e-6
IN_COLS = 2304
LOG2_E = math.log2(math.e)
FOLD_ROWS = 64
ONES_ROWS = 16
C_QA, C_QC, C_KA, C_VA, C_KC, C_VC, C_UV, C_GLU = 0, 256, 512, 640, 768, 1024, 1280, 1792

VMEM_LIMIT_BYTES = 56 * 1024 * 1024


def _params(n_grid_dims):
    return pltpu.CompilerParams(
        dimension_semantics=("parallel",) * n_grid_dims,
        vmem_limit_bytes=VMEM_LIMIT_BYTES)


def _resident(shape):
    zeros = (0,) * len(shape)
    return pl.BlockSpec(shape, lambda *_: zeros, pipeline_mode=pl.Buffered(1))


def _rms(x, axis=-1):
    return x * lax.rsqrt(jnp.mean(x * x, axis=axis, keepdims=True) + EPS)


def _modulated_norm(x, g, mod, base):
    shift = mod[base:base + 1, :]
    scale = mod[base + 1:base + 2, :]
    return (_rms(x) * g) * (1.0 + scale) + shift


def _ada_kernel(c_ref, w_ref, b_ref, o_ref):
    c = c_ref[...]
    s = (c * jax.nn.sigmoid(c)).astype(BF16)
    o_ref[0] = jnp.dot(s, w_ref[0].astype(BF16), preferred_element_type=F32) + b_ref[0]


def _ada(cc, w_ada, b_ada):
    depth, d, n = w_ada.shape
    rows = cc.shape[0]
    tn = 1024
    return pl.pallas_call(
        _ada_kernel,
        out_shape=jax.ShapeDtypeStruct((depth, rows, n), F32),
        grid=(depth, n // tn),
        in_specs=[
            pl.BlockSpec((rows, d), lambda l, j: (0, 0)),
            pl.BlockSpec((1, d, tn), lambda l, j: (l, 0, j)),
            pl.BlockSpec((1, 1, tn), lambda l, j: (l, 0, j)),
        ],
        out_specs=pl.BlockSpec((1, rows, tn), lambda l, j: (l, 0, j)),
        compiler_params=_params(2),
        name="adaln",
    )(cc, w_ada, b_ada.reshape(depth, 1, n))


def _ffn_kernel(*refs, mod_base, final):
    if final:
        x_ref, mod_ref, g_ref, wab_ref, wo_ref, gf_ref, o_ref, xb_ref, hm_ref = refs
    else:
        x_ref, mod_ref, g_ref, wab_ref, wo_ref, o_ref, xb_ref, hm_ref = refs
    mod = mod_ref[0]
    gate = mod[mod_base + 2:mod_base + 3, :]
    tm = x_ref.shape[1]
    half = tm // FFN_ROW_SPLIT
    for r in range(0, tm, half):
        rows = slice(r, r + half)
        x = x_ref[0, rows, :]
        xb_ref[rows, :] = _modulated_norm(x, g_ref[...], mod, mod_base).astype(BF16)
        for c in range(N_FF_CHUNKS):
            ab = jnp.dot(xb_ref[rows, :], wab_ref[c], preferred_element_type=F32)
            a = ab[:, :FF_CHUNK]
            b = ab[:, FF_CHUNK:]
            hm_ref[rows, c * FF_CHUNK:(c + 1) * FF_CHUNK] = ((a * jax.nn.sigmoid(a)) * b).astype(BF16)
        y = jnp.dot(hm_ref[rows, :], wo_ref[...], preferred_element_type=F32)
        out = x + (0.5 * gate) * y
        if final:
            out = _rms(out) * gf_ref[...]
        o_ref[0, rows, :] = out


def _ffn(x, mod, g, wab, wo, mod_base, tm, g_final=None):
    nb, n, d = x.shape
    tm = min(tm, n)
    per_batch_mod = mod.shape[0] == nb and nb > 1
    mod_map = (lambda b, i: (b, 0, 0)) if per_batch_mod else (lambda b, i: (0, 0, 0))
    final = g_final is not None
    in_specs = [
        pl.BlockSpec((1, tm, d), lambda b, i: (b, i, 0)),
        pl.BlockSpec((1, N_MOD, d), mod_map),
        _resident((1, d)),
        _resident(wab.shape),
        _resident(wo.shape),
    ]
    args = [x, mod, g.reshape(1, d), wab, wo]
    if final:
        in_specs.append(_resident((1, d)))
        args.append(g_final.reshape(1, d))
    return pl.pallas_call(
        functools.partial(_ffn_kernel, mod_base=mod_base, final=final),
        out_shape=jax.ShapeDtypeStruct(x.shape, F32),
        grid=(nb, n // tm),
        in_specs=in_specs,
        out_specs=pl.BlockSpec((1, tm, d), lambda b, i: (b, i, 0)),
        scratch_shapes=[pltpu.VMEM((tm, d), BF16), pltpu.VMEM((tm, D_FF), BF16)],
        compiler_params=_params(2),
        name="ffn",
    )(*args)


def _swap_halves(x, seg):
    half = seg // 2
    width = x.shape[-1]
    lane = lax.broadcasted_iota(jnp.int32, x.shape, x.ndim - 1)
    first = (lane & (seg - 1)) < half
    return jnp.where(first, pltpu.roll(x, width - half, x.ndim - 1), pltpu.roll(x, half, x.ndim - 1))


def _segment_mean_sq(x, seg_ones, seg):
    sq = x * x
    hi = sq.astype(BF16)
    lo = (sq - hi.astype(F32)).astype(BF16)
    tot = (jnp.dot(hi, seg_ones, preferred_element_type=F32)
           + jnp.dot(lo, seg_ones, preferred_element_type=F32))
    return tot * (1.0 / seg)


def _gelu_tanh(x):
    cdf = 0.5 * (1.0 + jnp.tanh(math.sqrt(2.0 / math.pi) * (x + 0.044715 * (x * x * x))))
    return x * cdf


def _proj_kernel(x_ref, mod_ref, g_ref, w_ref, gq_ref, gk_ref, gv_ref, ws_ref, bs_ref,
                 cosa_ref, sina_ref, cosc_ref, sinc_ref, seg_ref,
                 qa_ref, qc_ref, ka_ref, vta_ref, kc_ref, vtc_ref, ob_ref, yc_ref, *, rope):
    x = x_ref[0]
    tm = x.shape[0]
    xb = _modulated_norm(x, g_ref[...], mod_ref[0], 3).astype(BF16)
    y = jnp.dot(xb, w_ref[...], preferred_element_type=F32)
    seg = seg_ref[...]

    def rope_a(v, width):
        if not rope:
            return v
        return v * cosa_ref[:, :width] + _swap_halves(v, HEAD_DIM) * sina_ref[:, :width]

    def rope_c(v):
        if not rope:
            return v
        return v * cosc_ref[...] + _swap_halves(v, DIFF_DIM) * sinc_ref[...]

    qa = y[:, C_QA:C_QA + 256]
    qa = qa * lax.rsqrt(_segment_mean_sq(qa, seg, HEAD_DIM) + EPS) * gq_ref[...]
    qa_ref[0] = (rope_a(qa, 256) * (HEAD_DIM ** -0.5 * LOG2_E)).astype(BF16)
    ka = y[:, C_KA:C_KA + 128]
    ka = ka * lax.rsqrt(_segment_mean_sq(ka, seg[:128, :128], HEAD_DIM) + EPS) * gk_ref[...]
    ka_ref[0] = rope_a(ka, 128).astype(BF16)
    vta_ref[0] = y[:, C_VA:C_VA + 128].T.astype(BF16)

    qc_ref[0] = (rope_c(y[:, C_QC:C_QC + 256]) * (DIFF_DIM ** -0.5 * LOG2_E)).astype(BF16)
    kc_ref[0] = rope_c(y[:, C_KC:C_KC + 256]).astype(BF16)
    vtc_ref[0] = y[:, C_VC:C_VC + 256].T.astype(BF16)

    uv = _gelu_tanh(y[:, C_UV:C_UV + 512])
    u = uv[:, :256]
    vn = (_rms(uv[:, 256:]) * gv_ref[...]).astype(BF16)
    group = lax.broadcasted_iota(jnp.int32, (CHUNK, GROUP_WIDTH), 1) // HEAD_DIM
    for j in range(tm // CHUNK):
        rows = slice(j * CHUNK, (j + 1) * CHUNK)
        r = jnp.dot(ws_ref[...], vn[rows], preferred_element_type=F32)
        mixed = r[0:CHUNK]
        for gi in range(1, 4):
            mixed = jnp.where(group == gi, r[gi * CHUNK:(gi + 1) * CHUNK], mixed)
        ob_ref[0, rows, :] = (u[rows] * (mixed + bs_ref[...])).astype(BF16)

    glu = y[:, C_GLU:C_GLU + 512]
    yc_ref[0] = glu[:, :256] * jax.nn.sigmoid(glu[:, 256:])


def _proj(x, mod, g, w_in, p, tables, tm, rope):
    nb, n, d = x.shape
    tm = min(tm, n)
    per_batch_mod = mod.shape[0] == nb and nb > 1
    mod_map = (lambda b, i: (b, 0, 0)) if per_batch_mod else (lambda b, i: (0, 0, 0))
    tok = lambda w: pl.BlockSpec((1, tm, w), lambda b, i: (b, i, 0))
    tokt = lambda w: pl.BlockSpec((1, w, tm), lambda b, i: (b, 0, i))
    tab = lambda w: pl.BlockSpec((tm, w), lambda b, i: (i, 0))
    cosa, sina, cosc, sinc = tables
    out_shape = [
        jax.ShapeDtypeStruct((nb, n, 256), BF16),
        jax.ShapeDtypeStruct((nb, n, 256), BF16),
        jax.ShapeDtypeStruct((nb, n, 128), BF16),
        jax.ShapeDtypeStruct((nb, 128, n), BF16),
        jax.ShapeDtypeStruct((nb, n, 256), BF16),
        jax.ShapeDtypeStruct((nb, 256, n), BF16),
        jax.ShapeDtypeStruct((nb, n, 256), BF16),
        jax.ShapeDtypeStruct((nb, n, 256), F32),
    ]
    return pl.pallas_call(
        functools.partial(_proj_kernel, rope=rope),
        out_shape=out_shape,
        grid=(nb, n // tm),
        in_specs=[
            tok(d),
            pl.BlockSpec((1, N_MOD, d), mod_map),
            _resident((1, d)),
            _resident(w_in.shape),
            _resident((1, 256)), _resident((1, 128)), _resident((1, 256)),
            _resident(p["ws"].shape), _resident((CHUNK, 256)),
            tab(256), tab(256), tab(256), tab(256),
            _resident((256, 256)),
        ],
        out_specs=[tok(256), tok(256), tok(128), tokt(128), tok(256), tokt(256), tok(256), tok(256)],
        compiler_params=_params(2),
        name="proj",
    )(x, mod, g.reshape(1, d), w_in, p["gq"], p["gk"], p["gv"], p["ws"], p["bs"],
      cosa, sina, cosc, sinc, p["seg"])


def _proj_kv_kernel(x_ref, mod_ref, g_ref, w_ref, gk_ref, seg_ref, ka_ref, vta_ref, kc_ref, vtc_ref):
    xb = _modulated_norm(x_ref[0], g_ref[...], mod_ref[0], 3).astype(BF16)
    y = jnp.dot(xb, w_ref[:, C_KA:C_VC + 256], preferred_element_type=F32)
    ka = y[:, 0:128]
    ka = ka * lax.rsqrt(_segment_mean_sq(ka, seg_ref[:128, :128], HEAD_DIM) + EPS) * gk_ref[...]
    ka_ref[0] = ka.astype(BF16)
    vta_ref[0] = y[:, 128:256].T.astype(BF16)
    kc_ref[0] = y[:, 256:512].astype(BF16)
    vtc_ref[0] = y[:, 512:768].T.astype(BF16)


def _proj_kv(x, mod, g, w_in, p, tm):
    nb, n, d = x.shape
    tm = min(tm, n)
    tok = lambda w: pl.BlockSpec((1, tm, w), lambda b, i: (b, i, 0))
    tokt = lambda w: pl.BlockSpec((1, w, tm), lambda b, i: (b, 0, i))
    return pl.pallas_call(
        _proj_kv_kernel,
        out_shape=[
            jax.ShapeDtypeStruct((nb, n, 128), BF16), jax.ShapeDtypeStruct((nb, 128, n), BF16),
            jax.ShapeDtypeStruct((nb, n, 256), BF16), jax.ShapeDtypeStruct((nb, 256, n), BF16)],
        grid=(nb, n // tm),
        in_specs=[
            tok(d),
            pl.BlockSpec((1, N_MOD, d), lambda b, i: (0, 0, 0)),
            _resident((1, d)),
            _resident(w_in.shape),
            _resident((1, 128)),
            _resident((256, 256)),
        ],
        out_specs=[tok(128), tokt(128), tok(256), tokt(256)],
        compiler_params=_params(2),
        name="proj_kv",
    )(x, mod, g.reshape(1, d), w_in, p["gk"], p["seg"])


def _fold_rows(x, op, reduce_fn):
    n = x.shape[0]
    blk = min(FOLD_ROWS, n)
    acc = x[0:blk]
    for i in range(1, n // blk):
        acc = op(acc, x[i * blk:(i + 1) * blk])
    while blk > 8:
        blk //= 2
        acc = op(acc[:blk], acc[blk:2 * blk])
    return reduce_fn(acc, axis=0, keepdims=True)


def _scores_t(qm, ks):
    nt = (((1,), (1,)), ((), ()))
    return jnp.concatenate([lax.dot_general(k, qm, nt, preferred_element_type=F32) for k in ks], axis=0)


def _softmax_pv_t(st, vts):
    m = _fold_rows(st, jnp.maximum, jnp.max)
    p = jnp.exp2(st - m).astype(BF16)
    ot, row = None, 0
    for vt in vts:
        part = jnp.dot(vt, p[row:row + vt.shape[1]], preferred_element_type=F32)
        ot = part if ot is None else ot + part
        row += vt.shape[1]
    dv = ot.shape[0] - ONES_ROWS
    return ot[:dv], ot[dv:dv + 1]


def _attend_all_t(units):
    outs = []
    st = _scores_t(units[0][0], units[0][1])
    for i, (_, _, vts) in enumerate(units):
        st_next = _scores_t(units[i + 1][0], units[i + 1][1]) if i + 1 < len(units) else None
        outs.append(_softmax_pv_t(st, vts))
        st = st_next
    return outs


def _mix_kernel(*refs, lam_init, tq, n_kv):
    h_ref, mod_ref, qa_ref, qc_ref = refs[:4]
    kv_refs = [refs[4 + 4 * i:8 + 4 * i] for i in range(n_kv)]
    (ob_ref, yc_ref, wdw_ref, bdw_ref, gconv_ref, gsub_ref, lamc_ref, wout_ref,
     o_ref) = refs[4 + 4 * n_kv:]
    qi = pl.program_id(1)
    n_q = pl.num_programs(1)

    def masked(q, lo, width):
        lane = lax.broadcasted_iota(jnp.int32, q.shape, 1)
        keep = (lane >= lo) & (lane < lo + width)
        return jnp.where(keep, q, 0.0).astype(BF16)

    def with_ones(vt):
        return jnp.concatenate([vt, jnp.ones((ONES_ROWS, vt.shape[1]), BF16)], axis=0)

    qa = qa_ref[0].astype(F32)
    qc = qc_ref[0].astype(F32)
    kas = [kv[0][0] for kv in kv_refs]
    kcs = [kv[2][0] for kv in kv_refs]
    units = []
    for h in range(4):
        g, r = h // 2, h % 2
        vts = [with_ones(kv[1][0, HEAD_DIM * g:HEAD_DIM * (g + 1), :]) for kv in kv_refs]
        units.append((masked(qa[:, 128 * r:128 * (r + 1)], HEAD_DIM * g, HEAD_DIM), kas, vts))
    for h in range(4):
        vts = [with_ones(kv[3][0, HEAD_DIM * h:HEAD_DIM * (h + 1), :]) for kv in kv_refs]
        for c in range(2):
            units.append((masked(qc, HEAD_DIM * h + DIFF_DIM * c, DIFF_DIM), kcs, vts))
    outs = _attend_all_t(units)

    oat = jnp.concatenate([ot / l for ot, l in outs[:4]], axis=0)

    lp = lamc_ref[...]
    lam = (jnp.exp(jnp.sum(lp[0:1] * lp[1:2], keepdims=True))
           - jnp.exp(jnp.sum(lp[2:3] * lp[3:4], keepdims=True)) + lam_init)
    oc = []
    for h in range(4):
        (o0, l0), (o1, l1) = outs[4 + 2 * h], outs[5 + 2 * h]
        o = o0 / l0 - lam * (o1 / l1)
        oc.append(_rms(o, axis=0) * gsub_ref[...] * (1.0 - lam_init))
    oct = jnp.concatenate(oc, axis=0)

    n = yc_ref.shape[1]
    start = pl.multiple_of(qi * tq, 8)
    above = pl.multiple_of(jnp.maximum(start - CONV_PAD, 0), 8)
    below = pl.multiple_of(jnp.minimum(start + tq, n - CONV_PAD), 8)
    win = jnp.concatenate([
        jnp.where(qi > 0, yc_ref[0, pl.ds(above, CONV_PAD), :], 0.0),
        yc_ref[0, pl.ds(start, tq), :],
        jnp.where(qi < n_q - 1, yc_ref[0, pl.ds(below, CONV_PAD), :], 0.0)], axis=0)
    n_win = tq + 2 * CONV_PAD
    acc = None
    for b in range(8):
        shifted = win if b == 0 else pltpu.roll(win, n_win - b, 0)
        for k in range(CONV_K):
            off = CONV_PAD - CONV_K // 2 + k
            if off % 8 != b:
                continue
            term = shifted[off - b:off - b + tq] * wdw_ref[k:k + 1, :]
            acc = term if acc is None else acc + term
    od = _rms(acc + bdw_ref[...]) * gconv_ref[...]
    od = od * jax.nn.sigmoid(od)

    xcat = jnp.concatenate([oat.T.astype(BF16), oct.T.astype(BF16), ob_ref[0], od.astype(BF16)], axis=1)
    mix = jnp.dot(xcat, wout_ref[...], preferred_element_type=F32)
    o_ref[0] = h_ref[0] + mod_ref[0][5:6, :] * mix


def _mix(h, mod, qa, qc, kv_sets, ob, yc, p, w_out, lam_init, tq):
    nb, n, d = h.shape
    tq = min(tq, n)
    per_batch_mod = mod.shape[0] == nb and nb > 1
    mod_map = (lambda b, i: (b, 0, 0)) if per_batch_mod else (lambda b, i: (0, 0, 0))
    tok = lambda w: pl.BlockSpec((1, tq, w), lambda b, i: (b, i, 0))
    per_batch = lambda a: pl.BlockSpec((1,) + a.shape[1:], lambda b, i: (b, 0, 0))
    kv_flat = [a for kv in kv_sets for a in kv]
    return pl.pallas_call(
        functools.partial(_mix_kernel, lam_init=lam_init, tq=tq, n_kv=len(kv_sets)),
        out_shape=jax.ShapeDtypeStruct(h.shape, F32),
        grid=(nb, n // tq),
        in_specs=[
            tok(d),
            pl.BlockSpec((1, N_MOD, d), mod_map),
            tok(256), tok(256),
            *[per_batch(a) for a in kv_flat],
            tok(256),
            per_batch(yc),
            _resident((CONV_K, 256)), _resident((1, 256)), _resident((1, 256)),
            _resident((HEAD_DIM, 1)), _resident((4, DIFF_DIM)),
            _resident((d, d)),
        ],
        out_specs=tok(d),
        compiler_params=_params(2),
        name="mix",
    )(h, mod, qa, qc, *kv_flat, ob, yc,
      p["wdw"], p["bdw"], p["gconv"], p["gsub"], p["lamc"], w_out)


def _rope_tables(n):
    t = np.arange(n)
    row, col = (t // GRID_W).astype(np.float32), (t % GRID_W).astype(np.float32)

    def table(head_dim):
        quarter = head_dim // 4
        inv = (ROPE_THETA ** (-jnp.arange(quarter, dtype=F32) / quarter))
        ang = jnp.concatenate([jnp.asarray(row)[:, None] * inv, jnp.asarray(col)[:, None] * inv], axis=-1)
        cos, sin = jnp.cos(ang), jnp.sin(ang)
        reps = 256 // head_dim
        return (jnp.tile(jnp.concatenate([cos, cos], axis=-1), (1, reps)),
                jnp.tile(jnp.concatenate([-sin, sin], axis=-1), (1, reps)))

    cosa, sina = table(HEAD_DIM)
    cosc, sinc = table(DIFF_DIM)
    return cosa, sina, cosc, sinc


def _layer_params(l, w_ff1_in, w_ff1_out, w_ff2_in, w_ff2_out, w_in, w_out, g_q_a, g_k_a, lam_c,
                  g_sub_c, g_v_b, w_s_b, b_s_b, w_dw_d, b_dw_d, g_conv_d):
    def ff_in(w):
        a = w[:, :D_FF].reshape(D_MODEL, N_FF_CHUNKS, FF_CHUNK)
        b = w[:, D_FF:].reshape(D_MODEL, N_FF_CHUNKS, FF_CHUNK)
        return jnp.concatenate([a, b], axis=-1).transpose(1, 0, 2).astype(BF16)

    wi = w_in[l]
    wi = jnp.concatenate([wi[:, 0:64], wi[:, 128:192], wi[:, 64:128], wi[:, 192:256], wi[:, 256:]], axis=1)
    seg = np.kron(np.eye(256 // HEAD_DIM, dtype=np.float32), np.ones((HEAD_DIM, HEAD_DIM), np.float32))
    return dict(
        ff1_ab=ff_in(w_ff1_in[l]), ff1_o=w_ff1_out[l].astype(BF16),
        ff2_ab=ff_in(w_ff2_in[l]), ff2_o=w_ff2_out[l].astype(BF16),
        w_in=wi.astype(BF16), w_out=w_out[l].astype(BF16),
        gq=jnp.tile(g_q_a[l], 4).reshape(1, 256), gk=jnp.tile(g_k_a[l], 2).reshape(1, 128),
        gv=g_v_b[l].reshape(1, 256),
        ws=w_s_b[l].reshape(4 * CHUNK, CHUNK).astype(BF16),
        bs=jnp.repeat(b_s_b[l].T, HEAD_DIM, axis=1),
        seg=jnp.asarray(seg, BF16),
        wdw=w_dw_d[l], bdw=b_dw_d[l].reshape(1, 256), gconv=g_conv_d[l].reshape(1, 256),
        gsub=g_sub_c[l].reshape(HEAD_DIM, 1), lamc=lam_c[l],
    )


def kernel(x, c, ctx, c_ctx, w_ada, b_ada, g_norm, w_ff1_in, w_ff1_out, w_ff2_in, w_ff2_out, w_in, w_out, g_q_a, g_k_a, lam_c, g_sub_c, g_v_b, w_s_b, b_s_b, w_dw_d, b_dw_d, g_conv_d, g_final):
    nb, n, d = x.shape
    n_ctx = ctx.shape[1]
    depth = w_ada.shape[0]
    tm_ffn = 512
    tm_proj = 512
    tq = 1024

    rows = -(-(nb + 1) // 8) * 8
    cc = jnp.zeros((rows, d), F32).at[:nb].set(c).at[nb].set(c_ctx)
    mods = _ada(cc, w_ada, b_ada)
    tables = _rope_tables(n)
    tables_ctx = tuple(t[:n_ctx] for t in tables)

    h, hc = x, ctx
    for l in range(depth):
        last = l == depth - 1
        p = _layer_params(l, w_ff1_in, w_ff1_out, w_ff2_in, w_ff2_out, w_in, w_out, g_q_a, g_k_a,
                          lam_c, g_sub_c, g_v_b, w_s_b, b_s_b, w_dw_d, b_dw_d, g_conv_d)
        m = mods[l, :nb].reshape(nb, N_MOD, d)
        mc = mods[l, nb].reshape(1, N_MOD, d)
        lam_init = 0.8 - 0.6 * math.exp(-0.3 * l)

        h = _ffn(h, m, g_norm[l, 0], p["ff1_ab"], p["ff1_o"], 0, tm_ffn)
        hc = _ffn(hc.reshape(1, nb * n_ctx, d), mc, g_norm[l, 0], p["ff1_ab"], p["ff1_o"], 0,
                  tm_ffn).reshape(nb, n_ctx, d)

        qa, qc, ka, vta, kc, vtc, ob, yc = _proj(h, m, g_norm[l, 1], p["w_in"], p, tables, tm_proj, True)
        if last:
            kv_x = _proj_kv(hc, mc, g_norm[l, 1], p["w_in"], p, tm_proj)
        else:
            qa_x, qc_x, ka_x, vta_x, kc_x, vtc_x, ob_x, yc_x = _proj(
                hc, mc, g_norm[l, 1], p["w_in"], p, tables_ctx, tm_proj, False)
            kv_x = (ka_x, vta_x, kc_x, vtc_x)
        h = _mix(h, m, qa, qc, [(ka, vta, kc, vtc), kv_x], ob, yc, p, p["w_out"], lam_init, tq)
        if not last:
            hc = _mix(hc, mc, qa_x, qc_x, [kv_x], ob_x, yc_x, p, p["w_out"], lam_init, tq)

        h = _ffn(h, m, g_norm[l, 2], p["ff2_ab"], p["ff2_o"], 6, tm_ffn,
                 g_final=g_final if last else None)
        if not last:
            hc = _ffn(hc.reshape(1, nb * n_ctx, d), mc, g_norm[l, 2], p["ff2_ab"], p["ff2_o"], 6,
                      tm_ffn).reshape(nb, n_ctx, d)
    return h
```

```python
import functools
import math

import jax
import jax.numpy as jnp
import numpy as np
from jax import lax
from jax.experimental import pallas as pl
from jax.experimental.pallas import tpu as pltpu

F32 = jnp.float32
BF16 = jnp.bfloat16

D_MODEL = 1024
N_MOD = 9
HEAD_DIM = 64
DIFF_DIM = 32
GROUP_WIDTH = 256
D_FF = 2816
FF_CHUNK = 256
N_FF_CHUNKS = D_FF // FF_CHUNK
FFN_ROW_SPLIT = 2
CHUNK = 128
CONV_K = 31
CONV_PAD = 16
GRID_W = 64
ROPE_THETA = 10000.0
EPS = 1e-6
IN_COLS = 2304
LOG2_E = math.log2(math.e)
FOLD_ROWS = 64
ONES_ROWS = 16
C_QA, C_QC, C_KA, C_VA, C_KC, C_VC, C_UV, C_GLU = 0, 256, 512, 640, 768, 1024, 1280, 1792

VMEM_LIMIT_BYTES = 56 * 1024 * 1024


def _params(n_grid_dims):
    return pltpu.CompilerParams(
        dimension_semantics=("parallel",) * n_grid_dims,
        vmem_limit_bytes=VMEM_LIMIT_BYTES)


def _resident(shape):
    zeros = (0,) * len(shape)
    return pl.BlockSpec(shape, lambda *_: zeros, pipeline_mode=pl.Buffered(1))


def _rms(x, axis=-1):
    return x * lax.rsqrt(jnp.mean(x * x, axis=axis, keepdims=True) + EPS)


def _modulated_norm(x, g, mod, base):
    shift = mod[base:base + 1, :]
    scale = mod[base + 1:base + 2, :]
    return (_rms(x) * g) * (1.0 + scale) + shift


def _ada_kernel(c_ref, w_ref, b_ref, o_ref):
    c = c_ref[...]
    s = (c * jax.nn.sigmoid(c)).astype(BF16)
    o_ref[0] = jnp.dot(s, w_ref[0].astype(BF16), preferred_element_type=F32) + b_ref[0]


def _ada(cc, w_ada, b_ada):
    depth, d, n = w_ada.shape
    rows = cc.shape[0]
    tn = 1024
    return pl.pallas_call(
        _ada_kernel,
        out_shape=jax.ShapeDtypeStruct((depth, rows, n), F32),
        grid=(depth, n // tn),
        in_specs=[
            pl.BlockSpec((rows, d), lambda l, j: (0, 0)),
            pl.BlockSpec((1, d, tn), lambda l, j: (l, 0, j)),
            pl.BlockSpec((1, 1, tn), lambda l, j: (l, 0, j)),
        ],
        out_specs=pl.BlockSpec((1, rows, tn), lambda l, j: (l, 0, j)),
        compiler_params=_params(2),
        name="adaln",
    )(cc, w_ada, b_ada.reshape(depth, 1, n))


def _ffn_kernel(*refs, mod_base, final):
    if final:
        x_ref, mod_ref, g_ref, wab_ref, wo_ref, gf_ref, o_ref, xb_ref, hm_ref = refs
    else:
        x_ref, mod_ref, g_ref, wab_ref, wo_ref, o_ref, xb_ref, hm_ref = refs
    mod = mod_ref[0]
    gate = mod[mod_base + 2:mod_base + 3, :]
    tm = x_ref.shape[1]
    half = tm // FFN_ROW_SPLIT
    for r in range(0, tm, half):
        rows = slice(r, r + half)
        x = x_ref[0, rows, :]
        xb_ref[rows, :] = _modulated_norm(x, g_ref[...], mod, mod_base).astype(BF16)
        for c in range(N_FF_CHUNKS):
            ab = jnp.dot(xb_ref[rows, :], wab_ref[c], preferred_element_type=F32)
            a = ab[:, :FF_CHUNK]
            b = ab[:, FF_CHUNK:]
            hm_ref[rows, c * FF_CHUNK:(c + 1) * FF_CHUNK] = ((a * jax.nn.sigmoid(a)) * b).astype(BF16)
        y = jnp.dot(hm_ref[rows, :], wo_ref[...], preferred_element_type=F32)
        out = x + (0.5 * gate) * y
        if final:
            out = _rms(out) * gf_ref[...]
        o_ref[0, rows, :] = out


def _ffn(x, mod, g, wab, wo, mod_base, tm, g_final=None):
    nb, n, d = x.shape
    tm = min(tm, n)
    per_batch_mod = mod.shape[0] == nb and nb > 1
    mod_map = (lambda b, i: (b, 0, 0)) if per_batch_mod else (lambda b, i: (0, 0, 0))
    final = g_final is not None
    in_specs = [
        pl.BlockSpec((1, tm, d), lambda b, i: (b, i, 0)),
        pl.BlockSpec((1, N_MOD, d), mod_map),
        _resident((1, d)),
        _resident(wab.shape),
        _resident(wo.shape),
    ]
    args = [x, mod, g.reshape(1, d), wab, wo]
    if final:
        in_specs.append(_resident((1, d)))
        args.append(g_final.reshape(1, d))
    return pl.pallas_call(
        functools.partial(_ffn_kernel, mod_base=mod_base, final=final),
        out_shape=jax.ShapeDtypeStruct(x.shape, F32),
        grid=(nb, n // tm),
        in_specs=in_specs,
        out_specs=pl.BlockSpec((1, tm, d), lambda b, i: (b, i, 0)),
        scratch_shapes=[pltpu.VMEM((tm, d), BF16), pltpu.VMEM((tm, D_FF), BF16)],
        compiler_params=_params(2),
        name="ffn",
    )(*args)


def _swap_halves(x, seg):
    half = seg // 2
    width = x.shape[-1]
    lane = lax.broadcasted_iota(jnp.int32, x.shape, x.ndim - 1)
    first = (lane & (seg - 1)) < half
    return jnp.where(first, pltpu.roll(x, width - half, x.ndim - 1), pltpu.roll(x, half, x.ndim - 1))


def _segment_mean_sq(x, seg_ones, seg):
    sq = x * x
    hi = sq.astype(BF16)
    lo = (sq - hi.astype(F32)).astype(BF16)
    tot = (jnp.dot(hi, seg_ones, preferred_element_type=F32)
           + jnp.dot(lo, seg_ones, preferred_element_type=F32))
    return tot * (1.0 / seg)


def _gelu_tanh(x):
    cdf = 0.5 * (1.0 + jnp.tanh(math.sqrt(2.0 / math.pi) * (x + 0.044715 * (x * x * x))))
    return x * cdf


def _proj_kernel(x_ref, mod_ref, g_ref, w_ref, gq_ref, gk_ref, gv_ref, ws_ref, bs_ref,
                 cosa_ref, sina_ref, cosc_ref, sinc_ref, seg_ref,
                 qa_ref, qc_ref, ka_ref, vta_ref, kc_ref, vtc_ref, ob_ref, yc_ref, *, rope):
    x = x_ref[0]
    tm = x.shape[0]
    xb = _modulated_norm(x, g_ref[...], mod_ref[0], 3).astype(BF16)
    y = jnp.dot(xb, w_ref[...], preferred_element_type=F32)
    seg = seg_ref[...]

    def rope_a(v, width):
        if not rope:
            return v
        return v * cosa_ref[:, :width] + _swap_halves(v, HEAD_DIM) * sina_ref[:, :width]

    def rope_c(v):
        if not rope:
            return v
        return v * cosc_ref[...] + _swap_halves(v, DIFF_DIM) * sinc_ref[...]

    qa = y[:, C_QA:C_QA + 256]
    qa = qa * lax.rsqrt(_segment_mean_sq(qa, seg, HEAD_DIM) + EPS) * gq_ref[...]
    qa_ref[0] = (rope_a(qa, 256) * (HEAD_DIM ** -0.5 * LOG2_E)).astype(BF16)
    ka = y[:, C_KA:C_KA + 128]
    ka = ka * lax.rsqrt(_segment_mean_sq(ka, seg[:128, :128], HEAD_DIM) + EPS) * gk_ref[...]
    ka_ref[0] = rope_a(ka, 128).astype(BF16)
    vta_ref[0] = y[:, C_VA:C_VA + 128].T.astype(BF16)

    qc_ref[0] = (rope_c(y[:, C_QC:C_QC + 256]) * (DIFF_DIM ** -0.5 * LOG2_E)).astype(BF16)
    kc_ref[0] = rope_c(y[:, C_KC:C_KC + 256]).astype(BF16)
    vtc_ref[0] = y[:, C_VC:C_VC + 256].T.astype(BF16)

    uv = _gelu_tanh(y[:, C_UV:C_UV + 512])
    u = uv[:, :256]
    vn = (_rms(uv[:, 256:]) * gv_ref[...]).astype(BF16)
    group = lax.broadcasted_iota(jnp.int32, (CHUNK, GROUP_WIDTH), 1) // HEAD_DIM
    for j in range(tm // CHUNK):
        rows = slice(j * CHUNK, (j + 1) * CHUNK)
        r = jnp.dot(ws_ref[...], vn[rows], preferred_element_type=F32)
        mixed = r[0:CHUNK]
        for gi in range(1, 4):
            mixed = jnp.where(group == gi, r[gi * CHUNK:(gi + 1) * CHUNK], mixed)
        ob_ref[0, rows, :] = (u[rows] * (mixed + bs_ref[...])).astype(BF16)

    glu = y[:, C_GLU:C_GLU + 512]
    yc_ref[0] = glu[:, :256] * jax.nn.sigmoid(glu[:, 256:])


def _proj(x, mod, g, w_in, p, tables, tm, rope):
    nb, n, d = x.shape
    tm = min(tm, n)
    per_batch_mod = mod.shape[0] == nb and nb > 1
    mod_map = (lambda b, i: (b, 0, 0)) if per_batch_mod else (lambda b, i: (0, 0, 0))
    tok = lambda w: pl.BlockSpec((1, tm, w), lambda b, i: (b, i, 0))
    tokt = lambda w: pl.BlockSpec((1, w, tm), lambda b, i: (b, 0, i))
    tab = lambda w: pl.BlockSpec((tm, w), lambda b, i: (i, 0))
    cosa, sina, cosc, sinc = tables
    out_shape = [
        jax.ShapeDtypeStruct((nb, n, 256), BF16),
        jax.ShapeDtypeStruct((nb, n, 256), BF16),
        jax.ShapeDtypeStruct((nb, n, 128), BF16),
        jax.ShapeDtypeStruct((nb, 128, n), BF16),
        jax.ShapeDtypeStruct((nb, n, 256), BF16),
        jax.ShapeDtypeStruct((nb, 256, n), BF16),
        jax.ShapeDtypeStruct((nb, n, 256), BF16),
        jax.ShapeDtypeStruct((nb, n, 256), F32),
    ]
    return pl.pallas_call(
        functools.partial(_proj_kernel, rope=rope),
        out_shape=out_shape,
        grid=(nb, n // tm),
        in_specs=[
            tok(d),
            pl.BlockSpec((1, N_MOD, d), mod_map),
            _resident((1, d)),
            _resident(w_in.shape),
            _resident((1, 256)), _resident((1, 128)), _resident((1, 256)),
            _resident(p["ws"].shape), _resident((CHUNK, 256)),
            tab(256), tab(256), tab(256), tab(256),
            _resident((256, 256)),
        ],
        out_specs=[tok(256), tok(256), tok(128), tokt(128), tok(256), tokt(256), tok(256), tok(256)],
        compiler_params=_params(2),
        name="proj",
    )(x, mod, g.reshape(1, d), w_in, p["gq"], p["gk"], p["gv"], p["ws"], p["bs"],
      cosa, sina, cosc, sinc, p["seg"])


def _proj_kv_kernel(x_ref, mod_ref, g_ref, w_ref, gk_ref, seg_ref, ka_ref, vta_ref, kc_ref, vtc_ref):
    xb = _modulated_norm(x_ref[0], g_ref[...], mod_ref[0], 3).astype(BF16)
    y = jnp.dot(xb, w_ref[:, C_KA:C_VC + 256], preferred_element_type=F32)
    ka = y[:, 0:128]
    ka = ka * lax.rsqrt(_segment_mean_sq(ka, seg_ref[:128, :128], HEAD_DIM) + EPS) * gk_ref[...]
    ka_ref[0] = ka.astype(BF16)
    vta_ref[0] = y[:, 128:256].T.astype(BF16)
    kc_ref[0] = y[:, 256:512].astype(BF16)
    vtc_ref[0] = y[:, 512:768].T.astype(BF16)


def _proj_kv(x, mod, g, w_in, p, tm):
    nb, n, d = x.shape
    tm = min(tm, n)
    tok = lambda w: pl.BlockSpec((1, tm, w), lambda b, i: (b, i, 0))
    tokt = lambda w: pl.BlockSpec((1, w, tm), lambda b, i: (b, 0, i))
    return pl.pallas_call(
        _proj_kv_kernel,
        out_shape=[
            jax.ShapeDtypeStruct((nb, n, 128), BF16), jax.ShapeDtypeStruct((nb, 128, n), BF16),
            jax.ShapeDtypeStruct((nb, n, 256), BF16), jax.ShapeDtypeStruct((nb, 256, n), BF16)],
        grid=(nb, n // tm),
        in_specs=[
            tok(d),
            pl.BlockSpec((1, N_MOD, d), lambda b, i: (0, 0, 0)),
            _resident((1, d)),
            _resident(w_in.shape),
            _resident((1, 128)),
            _resident((256, 256)),
        ],
        out_specs=[tok(128), tokt(128), tok(256), tokt(256)],
        compiler_params=_params(2),
        name="proj_kv",
    )(x, mod, g.reshape(1, d), w_in, p["gk"], p["seg"])


def _fold_rows(x, op, reduce_fn):
    n = x.shape[0]
    blk = min(FOLD_ROWS, n)
    acc = x[0:blk]
    for i in range(1, n // blk):
        acc = op(acc, x[i * blk:(i + 1) * blk])
    while blk > 8:
        blk //= 2
        acc = op(acc[:blk], acc[blk:2 * blk])
    return reduce_fn(acc, axis=0, keepdims=True)


def _scores_t(qm, ks):
    nt = (((1,), (1,)), ((), ()))
    return jnp.concatenate([lax.dot_general(k, qm, nt, preferred_element_type=F32) for k in ks], axis=0)


def _softmax_pv_t(st, vts):
    m = _fold_rows(st, jnp.maximum, jnp.max)
    p = jnp.exp2(st - m).astype(BF16)
    ot, row = None, 0
    for vt in vts:
        part = jnp.dot(vt, p[row:row + vt.shape[1]], preferred_element_type=F32)
        ot = part if ot is None else ot + part
        row += vt.shape[1]
    dv = ot.shape[0] - ONES_ROWS
    return ot[:dv], ot[dv:dv + 1]


def _attend_all_t(units):
    outs = []
    st = _scores_t(units[0][0], units[0][1])
    for i, (_, _, vts) in enumerate(units):
        st_next = _scores_t(units[i + 1][0], units[i + 1][1]) if i + 1 < len(units) else None
        outs.append(_softmax_pv_t(st, vts))
        st = st_next
    return outs


def _mix_kernel(*refs, lam_init, tq, n_kv):
    h_ref, mod_ref, qa_ref, qc_ref = refs[:4]
    kv_refs = [refs[4 + 4 * i:8 + 4 * i] for i in range(n_kv)]
    (ob_ref, yc_ref, wdw_ref, bdw_ref, gconv_ref, gsub_ref, lamc_ref, wout_ref,
     o_ref) = refs[4 + 4 * n_kv:]
    qi = pl.program_id(1)
    n_q = pl.num_programs(1)

    def masked(q, lo, width):
        lane = lax.broadcasted_iota(jnp.int32, q.shape, 1)
        keep = (lane >= lo) & (lane < lo + width)
        return jnp.where(keep, q, 0.0).astype(BF16)

    def with_ones(vt):
        return jnp.concatenate([vt, jnp.ones((ONES_ROWS, vt.shape[1]), BF16)], axis=0)

    qa = qa_ref[0].astype(F32)
    qc = qc_ref[0].astype(F32)
    kas = [kv[0][0] for kv in kv_refs]
    kcs = [kv[2][0] for kv in kv_refs]
    units = []
    for h in range(4):
        g, r = h // 2, h % 2
        vts = [with_ones(kv[1][0, HEAD_DIM * g:HEAD_DIM * (g + 1), :]) for kv in kv_refs]
        units.append((masked(qa[:, 128 * r:128 * (r + 1)], HEAD_DIM * g, HEAD_DIM), kas, vts))
    for h in range(4):
        vts = [with_ones(kv[3][0, HEAD_DIM * h:HEAD_DIM * (h + 1), :]) for kv in kv_refs]
        for c in range(2):
            units.append((masked(qc, HEAD_DIM * h + DIFF_DIM * c, DIFF_DIM), kcs, vts))
    outs = _attend_all_t(units)

    oat = jnp.concatenate([ot / l for ot, l in outs[:4]], axis=0)

    lp = lamc_ref[...]
    lam = (jnp.exp(jnp.sum(lp[0:1] * lp[1:2], keepdims=True))
           - jnp.exp(jnp.sum(lp[2:3] * lp[3:4], keepdims=True)) + lam_init)
    oc = []
    for h in range(4):
        (o0, l0), (o1, l1) = outs[4 + 2 * h], outs[5 + 2 * h]
        o = o0 / l0 - lam * (o1 / l1)
        oc.append(_rms(o, axis=0) * gsub_ref[...] * (1.0 - lam_init))
    oct = jnp.concatenate(oc, axis=0)

    n = yc_ref.shape[1]
    start = pl.multiple_of(qi * tq, 8)
    above = pl.multiple_of(jnp.maximum(start - CONV_PAD, 0), 8)
    below = pl.multiple_of(jnp.minimum(start + tq, n - CONV_PAD), 8)
    win = jnp.concatenate([
        jnp.where(qi > 0, yc_ref[0, pl.ds(above, CONV_PAD), :], 0.0),
        yc_ref[0, pl.ds(start, tq), :],
        jnp.where(qi < n_q - 1, yc_ref[0, pl.ds(below, CONV_PAD), :], 0.0)], axis=0)
    n_win = tq + 2 * CONV_PAD
    acc = None
    for b in range(8):
        shifted = win if b == 0 else pltpu.roll(win, n_win - b, 0)
        for k in range(CONV_K):
            off = CONV_PAD - CONV_K // 2 + k
            if off % 8 != b:
                continue
            term = shifted[off - b:off - b + tq] * wdw_ref[k:k + 1, :]
            acc = term if acc is None else acc + term
    od = _rms(acc + bdw_ref[...]) * gconv_ref[...]
    od = od * jax.nn.sigmoid(od)

    xcat = jnp.concatenate([oat.T.astype(BF16), oct.T.astype(BF16), ob_ref[0], od.astype(BF16)], axis=1)
    mix = jnp.dot(xcat, wout_ref[...], preferred_element_type=F32)
    o_ref[0] = h_ref[0] + mod_ref[0][5:6, :] * mix


def _mix(h, mod, qa, qc, kv_sets, ob, yc, p, w_out, lam_init, tq):
    nb, n, d = h.shape
    tq = min(tq, n)
    per_batch_mod = mod.shape[0] == nb and nb > 1
    mod_map = (lambda b, i: (b, 0, 0)) if per_batch_mod else (lambda b, i: (0, 0, 0))
    tok = lambda w: pl.BlockSpec((1, tq, w), lambda b, i: (b, i, 0))
    per_batch = lambda a: pl.BlockSpec((1,) + a.shape[1:], lambda b, i: (b, 0, 0))
    kv_flat = [a for kv in kv_sets for a in kv]
    return pl.pallas_call(
        functools.partial(_mix_kernel, lam_init=lam_init, tq=tq, n_kv=len(kv_sets)),
        out_shape=jax.ShapeDtypeStruct(h.shape, F32),
        grid=(nb, n // tq),
        in_specs=[
            tok(d),
            pl.BlockSpec((1, N_MOD, d), mod_map),
            tok(256), tok(256),
            *[per_batch(a) for a in kv_flat],
            tok(256),
            per_batch(yc),
            _resident((CONV_K, 256)), _resident((1, 256)), _resident((1, 256)),
            _resident((HEAD_DIM, 1)), _resident((4, DIFF_DIM)),
            _resident((d, d)),
        ],
        out_specs=tok(d),
        compiler_params=_params(2),
        name="mix",
    )(h, mod, qa, qc, *kv_flat, ob, yc,
      p["wdw"], p["bdw"], p["gconv"], p["gsub"], p["lamc"], w_out)


def _rope_tables(n):
    t = np.arange(n)
    row, col = (t // GRID_W).astype(np.float32), (t % GRID_W).astype(np.float32)

    def table(head_dim):
        quarter = head_dim // 4
        inv = (ROPE_THETA ** (-jnp.arange(quarter, dtype=F32) / quarter))
        ang = jnp.concatenate([jnp.asarray(row)[:, None] * inv, jnp.asarray(col)[:, None] * inv], axis=-1)
        cos, sin = jnp.cos(ang), jnp.sin(ang)
        reps = 256 // head_dim
        return (jnp.tile(jnp.concatenate([cos, cos], axis=-1), (1, reps)),
                jnp.tile(jnp.concatenate([-sin, sin], axis=-1), (1, reps)))

    cosa, sina = table(HEAD_DIM)
    cosc, sinc = table(DIFF_DIM)
    return cosa, sina, cosc, sinc


def _layer_params(l, w_ff1_in, w_ff1_out, w_ff2_in, w_ff2_out, w_in, w_out, g_q_a, g_k_a, lam_c,
                  g_sub_c, g_v_b, w_s_b, b_s_b, w_dw_d, b_dw_d, g_conv_d):
    def ff_in(w):
        a = w[:, :D_FF].reshape(D_MODEL, N_FF_CHUNKS, FF_CHUNK)
        b = w[:, D_FF:].reshape(D_MODEL, N_FF_CHUNKS, FF_CHUNK)
        return jnp.concatenate([a, b], axis=-1).transpose(1, 0, 2).astype(BF16)

    wi = w_in[l]
    wi = jnp.concatenate([wi[:, 0:64], wi[:, 128:192], wi[:, 64:128], wi[:, 192:256], wi[:, 256:]], axis=1)
    seg = np.kron(np.eye(256 // HEAD_DIM, dtype=np.float32), np.ones((HEAD_DIM, HEAD_DIM), np.float32))
    return dict(
        ff1_ab=ff_in(w_ff1_in[l]), ff1_o=w_ff1_out[l].astype(BF16),
        ff2_ab=ff_in(w_ff2_in[l]), ff2_o=w_ff2_out[l].astype(BF16),
        w_in=wi.astype(BF16), w_out=w_out[l].astype(BF16),
        gq=jnp.tile(g_q_a[l], 4).reshape(1, 256), gk=jnp.tile(g_k_a[l], 2).reshape(1, 128),
        gv=g_v_b[l].reshape(1, 256),
        ws=w_s_b[l].reshape(4 * CHUNK, CHUNK).astype(BF16),
        bs=jnp.repeat(b_s_b[l].T, HEAD_DIM, axis=1),
        seg=jnp.asarray(seg, BF16),
        wdw=w_dw_d[l], bdw=b_dw_d[l].reshape(1, 256), gconv=g_conv_d[l].reshape(1, 256),
        gsub=g_sub_c[l].reshape(HEAD_DIM, 1), lamc=lam_c[l],
    )


def kernel(x, c, ctx, c_ctx, w_ada, b_ada, g_norm, w_ff1_in, w_ff1_out, w_ff2_in, w_ff2_out, w_in, w_out, g_q_a, g_k_a, lam_c, g_sub_c, g_v_b, w_s_b, b_s_b, w_dw_d, b_dw_d, g_conv_d, g_final):
    nb, n, d = x.shape
    n_ctx = ctx.shape[1]
    depth = w_ada.shape[0]
    tm_ffn = 1024
    tm_proj = 512
    tq = 1024

    rows = -(-(nb + 1) // 8) * 8
    cc = jnp.zeros((rows, d), F32).at[:nb].set(c).at[nb].set(c_ctx)
    mods = _ada(cc, w_ada, b_ada)
    tables = _rope_tables(n)
    tables_ctx = tuple(t[:n_ctx] for t in tables)

    h, hc = x, ctx
    for l in range(depth):
        last = l == depth - 1
        p = _layer_params(l, w_ff1_in, w_ff1_out, w_ff2_in, w_ff2_out, w_in, w_out, g_q_a, g_k_a,
                          lam_c, g_sub_c, g_v_b, w_s_b, b_s_b, w_dw_d, b_dw_d, g_conv_d)
        m = mods[l, :nb].reshape(nb, N_MOD, d)
        mc = mods[l, nb].reshape(1, N_MOD, d)
        lam_init = 0.8 - 0.6 * math.exp(-0.3 * l)

        h = _ffn(h, m, g_norm[l, 0], p["ff1_ab"], p["ff1_o"], 0, tm_ffn)
        hc = _ffn(hc.reshape(1, nb * n_ctx, d), mc, g_norm[l, 0], p["ff1_ab"], p["ff1_o"], 0,
                  tm_ffn).reshape(nb, n_ctx, d)

        qa, qc, ka, vta, kc, vtc, ob, yc = _proj(h, m, g_norm[l, 1], p["w_in"], p, tables, tm_proj, True)
        if last:
            kv_x = _proj_kv(hc, mc, g_norm[l, 1], p["w_in"], p, tm_proj)
        else:
            qa_x, qc_x, ka_x, vta_x, kc_x, vtc_x, ob_x, yc_x = _proj(
                hc, mc, g_norm[l, 1], p["w_in"], p, tables_ctx, tm_proj, False)
            kv_x = (ka_x, vta_x, kc_x, vtc_x)
        h = _mix(h, m, qa, qc, [(ka, vta, kc, vtc), kv_x], ob, yc, p, p["w_out"], lam_init, tq)
        if not last:
            hc = _mix(hc, mc, qa_x, qc_x, [kv_x], ob_x, yc_x, p, p["w_out"], lam_init, tq)

        h = _ffn(h, m, g_norm[l, 2], p["ff2_ab"], p["ff2_o"], 6, tm_ffn,
                 g_final=g_final if last else None)
        if not last:
            hc = _ffn(hc.reshape(1, nb * n_ctx, d), mc, g_norm[l, 2], p["ff2_ab"], p["ff2_o"], 6,
                      tm_ffn).reshape(nb, n_ctx, d)
    return h
```

```python
import functools
import math

import jax
import jax.numpy as jnp
import numpy as np
from jax import lax
from jax.experimental import pallas as pl
from jax.experimental.pallas import tpu as pltpu

F32 = jnp.float32
BF16 = jnp.bfloat16

D_MODEL = 1024
N_MOD = 9
HEAD_DIM = 64
DIFF_DIM = 32
GROUP_WIDTH = 256
D_FF = 2816
FF_CHUNK = 256
N_FF_CHUNKS = D_FF // FF_CHUNK
FFN_ROW_SPLIT = 4
CHUNK = 128
CONV_K = 31
CONV_PAD = 16
GRID_W = 64
ROPE_THETA = 10000.0
EPS = 1e-6
IN_COLS = 2304
LOG2_E = math.log2(math.e)
FOLD_ROWS = 64
ONES_ROWS = 16
C_QA, C_QC, C_KA, C_VA, C_KC, C_VC, C_UV, C_GLU = 0, 256, 512, 640, 768, 1024, 1280, 1792

VMEM_LIMIT_BYTES = 56 * 1024 * 1024


def _params(n_grid_dims):
    return pltpu.CompilerParams(
        dimension_semantics=("parallel",) * n_grid_dims,
        vmem_limit_bytes=VMEM_LIMIT_BYTES)


def _resident(shape):
    zeros = (0,) * len(shape)
    return pl.BlockSpec(shape, lambda *_: zeros, pipeline_mode=pl.Buffered(1))


def _rms(x, axis=-1):
    return x * lax.rsqrt(jnp.mean(x * x, axis=axis, keepdims=True) + EPS)


def _modulated_norm(x, g, mod, base):
    shift = mod[base:base + 1, :]
    scale = mod[base + 1:base + 2, :]
    return (_rms(x) * g) * (1.0 + scale) + shift


def _ada_kernel(c_ref, w_ref, b_ref, o_ref):
    c = c_ref[...]
    s = (c * jax.nn.sigmoid(c)).astype(BF16)
    o_ref[0] = jnp.dot(s, w_ref[0].astype(BF16), preferred_element_type=F32) + b_ref[0]


def _ada(cc, w_ada, b_ada):
    depth, d, n = w_ada.shape
    rows = cc.shape[0]
    tn = 1024
    return pl.pallas_call(
        _ada_kernel,
        out_shape=jax.ShapeDtypeStruct((depth, rows, n), F32),
        grid=(depth, n // tn),
        in_specs=[
            pl.BlockSpec((rows, d), lambda l, j: (0, 0)),
            pl.BlockSpec((1, d, tn), lambda l, j: (l, 0, j)),
            pl.BlockSpec((1, 1, tn), lambda l, j: (l, 0, j)),
        ],
        out_specs=pl.BlockSpec((1, rows, tn), lambda l, j: (l, 0, j)),
        compiler_params=_params(2),
        name="adaln",
    )(cc, w_ada, b_ada.reshape(depth, 1, n))


def _ffn_kernel(*refs, mod_base, final):
    if final:
        x_ref, mod_ref, g_ref, wab_ref, wo_ref, gf_ref, o_ref, xb_ref, hm_ref = refs
    else:
        x_ref, mod_ref, g_ref, wab_ref, wo_ref, o_ref, xb_ref, hm_ref = refs
    mod = mod_ref[0]
    gate = mod[mod_base + 2:mod_base + 3, :]
    tm = x_ref.shape[1]
    half = tm // FFN_ROW_SPLIT
    for r in range(0, tm, half):
        rows = slice(r, r + half)
        x = x_ref[0, rows, :]
        xb_ref[rows, :] = _modulated_norm(x, g_ref[...], mod, mod_base).astype(BF16)
        for c in range(N_FF_CHUNKS):
            ab = jnp.dot(xb_ref[rows, :], wab_ref[c], preferred_element_type=F32)
            a = ab[:, :FF_CHUNK]
            b = ab[:, FF_CHUNK:]
            hm_ref[rows, c * FF_CHUNK:(c + 1) * FF_CHUNK] = ((a * jax.nn.sigmoid(a)) * b).astype(BF16)
        y = jnp.dot(hm_ref[rows, :], wo_ref[...], preferred_element_type=F32)
        out = x + (0.5 * gate) * y
        if final:
            out = _rms(out) * gf_ref[...]
        o_ref[0, rows, :] = out


def _ffn(x, mod, g, wab, wo, mod_base, tm, g_final=None):
    nb, n, d = x.shape
    tm = min(tm, n)
    per_batch_mod = mod.shape[0] == nb and nb > 1
    mod_map = (lambda b, i: (b, 0, 0)) if per_batch_mod else (lambda b, i: (0, 0, 0))
    final = g_final is not None
    in_specs = [
        pl.BlockSpec((1, tm, d), lambda b, i: (b, i, 0)),
        pl.BlockSpec((1, N_MOD, d), mod_map),
        _resident((1, d)),
        _resident(wab.shape),
        _resident(wo.shape),
    ]
    args = [x, mod, g.reshape(1, d), wab, wo]
    if final:
        in_specs.append(_resident((1, d)))
        args.append(g_final.reshape(1, d))
    return pl.pallas_call(
        functools.partial(_ffn_kernel, mod_base=mod_base, final=final),
        out_shape=jax.ShapeDtypeStruct(x.shape, F32),
        grid=(nb, n // tm),
        in_specs=in_specs,
        out_specs=pl.BlockSpec((1, tm, d), lambda b, i: (b, i, 0)),
        scratch_shapes=[pltpu.VMEM((tm, d), BF16), pltpu.VMEM((tm, D_FF), BF16)],
        compiler_params=_params(2),
        name="ffn",
    )(*args)


def _swap_halves(x, seg):
    half = seg // 2
    width = x.shape[-1]
    lane = lax.broadcasted_iota(jnp.int32, x.shape, x.ndim - 1)
    first = (lane & (seg - 1)) < half
    return jnp.where(first, pltpu.roll(x, width - half, x.ndim - 1), pltpu.roll(x, half, x.ndim - 1))


def _segment_mean_sq(x, seg_ones, seg):
    sq = x * x
    hi = sq.astype(BF16)
    lo = (sq - hi.astype(F32)).astype(BF16)
    tot = (jnp.dot(hi, seg_ones, preferred_element_type=F32)
           + jnp.dot(lo, seg_ones, preferred_element_type=F32))
    return tot * (1.0 / seg)


def _gelu_tanh(x):
    cdf = 0.5 * (1.0 + jnp.tanh(math.sqrt(2.0 / math.pi) * (x + 0.044715 * (x * x * x))))
    return x * cdf


def _proj_kernel(x_ref, mod_ref, g_ref, w_ref, gq_ref, gk_ref, gv_ref, ws_ref, bs_ref,
                 cosa_ref, sina_ref, cosc_ref, sinc_ref, seg_ref,
                 qa_ref, qc_ref, ka_ref, vta_ref, kc_ref, vtc_ref, ob_ref, yc_ref, *, rope):
    x = x_ref[0]
    tm = x.shape[0]
    xb = _modulated_norm(x, g_ref[...], mod_ref[0], 3).astype(BF16)
    y = jnp.dot(xb, w_ref[...], preferred_element_type=F32)
    seg = seg_ref[...]

    def rope_a(v, width):
        if not rope:
            return v
        return v * cosa_ref[:, :width] + _swap_halves(v, HEAD_DIM) * sina_ref[:, :width]

    def rope_c(v):
        if not rope:
            return v
        return v * cosc_ref[...] + _swap_halves(v, DIFF_DIM) * sinc_ref[...]

    qa = y[:, C_QA:C_QA + 256]
    qa = qa * lax.rsqrt(_segment_mean_sq(qa, seg, HEAD_DIM) + EPS) * gq_ref[...]
    qa_ref[0] = (rope_a(qa, 256) * (HEAD_DIM ** -0.5 * LOG2_E)).astype(BF16)
    ka = y[:, C_KA:C_KA + 128]
    ka = ka * lax.rsqrt(_segment_mean_sq(ka, seg[:128, :128], HEAD_DIM) + EPS) * gk_ref[...]
    ka_ref[0] = rope_a(ka, 128).astype(BF16)
    vta_ref[0] = y[:, C_VA:C_VA + 128].T.astype(BF16)

    qc_ref[0] = (rope_c(y[:, C_QC:C_QC + 256]) * (DIFF_DIM ** -0.5 * LOG2_E)).astype(BF16)
    kc_ref[0] = rope_c(y[:, C_KC:C_KC + 256]).astype(BF16)
    vtc_ref[0] = y[:, C_VC:C_VC + 256].T.astype(BF16)

    uv = _gelu_tanh(y[:, C_UV:C_UV + 512])
    u = uv[:, :256]
    vn = (_rms(uv[:, 256:]) * gv_ref[...]).astype(BF16)
    group = lax.broadcasted_iota(jnp.int32, (CHUNK, GROUP_WIDTH), 1) // HEAD_DIM
    for j in range(tm // CHUNK):
        rows = slice(j * CHUNK, (j + 1) * CHUNK)
        r = jnp.dot(ws_ref[...], vn[rows], preferred_element_type=F32)
        mixed = r[0:CHUNK]
        for gi in range(1, 4):
            mixed = jnp.where(group == gi, r[gi * CHUNK:(gi + 1) * CHUNK], mixed)
        ob_ref[0, rows, :] = (u[rows] * (mixed + bs_ref[...])).astype(BF16)

    glu = y[:, C_GLU:C_GLU + 512]
    yc_ref[0] = glu[:, :256] * jax.nn.sigmoid(glu[:, 256:])


def _proj(x, mod, g, w_in, p, tables, tm, rope):
    nb, n, d = x.shape
    tm = min(tm, n)
    per_batch_mod = mod.shape[0] == nb and nb > 1
    mod_map = (lambda b, i: (b, 0, 0)) if per_batch_mod else (lambda b, i: (0, 0, 0))
    tok = lambda w: pl.BlockSpec((1, tm, w), lambda b, i: (b, i, 0))
    tokt = lambda w: pl.BlockSpec((1, w, tm), lambda b, i: (b, 0, i))
    tab = lambda w: pl.BlockSpec((tm, w), lambda b, i: (i, 0))
    cosa, sina, cosc, sinc = tables
    out_shape = [
        jax.ShapeDtypeStruct((nb, n, 256), BF16),
        jax.ShapeDtypeStruct((nb, n, 256), BF16),
        jax.ShapeDtypeStruct((nb, n, 128), BF16),
        jax.ShapeDtypeStruct((nb, 128, n), BF16),
        jax.ShapeDtypeStruct((nb, n, 256), BF16),
        jax.ShapeDtypeStruct((nb, 256, n), BF16),
        jax.ShapeDtypeStruct((nb, n, 256), BF16),
        jax.ShapeDtypeStruct((nb, n, 256), F32),
    ]
    return pl.pallas_call(
        functools.partial(_proj_kernel, rope=rope),
        out_shape=out_shape,
        grid=(nb, n // tm),
        in_specs=[
            tok(d),
            pl.BlockSpec((1, N_MOD, d), mod_map),
            _resident((1, d)),
            _resident(w_in.shape),
            _resident((1, 256)), _resident((1, 128)), _resident((1, 256)),
            _resident(p["ws"].shape), _resident((CHUNK, 256)),
            tab(256), tab(256), tab(256), tab(256),
            _resident((256, 256)),
        ],
        out_specs=[tok(256), tok(256), tok(128), tokt(128), tok(256), tokt(256), tok(256), tok(256)],
        compiler_params=_params(2),
        name="proj",
    )(x, mod, g.reshape(1, d), w_in, p["gq"], p["gk"], p["gv"], p["ws"], p["bs"],
      cosa, sina, cosc, sinc, p["seg"])


def _proj_kv_kernel(x_ref, mod_ref, g_ref, w_ref, gk_ref, seg_ref, ka_ref, vta_ref, kc_ref, vtc_ref):
    xb = _modulated_norm(x_ref[0], g_ref[...], mod_ref[0], 3).astype(BF16)
    y = jnp.dot(xb, w_ref[:, C_KA:C_VC + 256], preferred_element_type=F32)
    ka = y[:, 0:128]
    ka = ka * lax.rsqrt(_segment_mean_sq(ka, seg_ref[:128, :128], HEAD_DIM) + EPS) * gk_ref[...]
    ka_ref[0] = ka.astype(BF16)
    vta_ref[0] = y[:, 128:256].T.astype(BF16)
    kc_ref[0] = y[:, 256:512].astype(BF16)
    vtc_ref[0] = y[:, 512:768].T.astype(BF16)


def _proj_kv(x, mod, g, w_in, p, tm):
    nb, n, d = x.shape
    tm = min(tm, n)
    tok = lambda w: pl.BlockSpec((1, tm, w), lambda b, i: (b, i, 0))
    tokt = lambda w: pl.BlockSpec((1, w, tm), lambda b, i: (b, 0, i))
    return pl.pallas_call(
        _proj_kv_kernel,
        out_shape=[
            jax.ShapeDtypeStruct((nb, n, 128), BF16), jax.ShapeDtypeStruct((nb, 128, n), BF16),
            jax.ShapeDtypeStruct((nb, n, 256), BF16), jax.ShapeDtypeStruct((nb, 256, n), BF16)],
        grid=(nb, n // tm),
        in_specs=[
            tok(d),
            pl.BlockSpec((1, N_MOD, d), lambda b, i: (0, 0, 0)),
            _resident((1, d)),
            _resident(w_in.shape),
            _resident((1, 128)),
            _resident((256, 256)),
        ],
        out_specs=[tok(128), tokt(128), tok(256), tokt(256)],
        compiler_params=_params(2),
        name="proj_kv",
    )(x, mod, g.reshape(1, d), w_in, p["gk"], p["seg"])


def _fold_rows(x, op, reduce_fn):
    n = x.shape[0]
    blk = min(FOLD_ROWS, n)
    acc = x[0:blk]
    for i in range(1, n // blk):
        acc = op(acc, x[i * blk:(i + 1) * blk])
    while blk > 8:
        blk //= 2
        acc = op(acc[:blk], acc[blk:2 * blk])
    return reduce_fn(acc, axis=0, keepdims=True)


def _scores_t(qm, ks):
    nt = (((1,), (1,)), ((), ()))
    return jnp.concatenate([lax.dot_general(k, qm, nt, preferred_element_type=F32) for k in ks], axis=0)


def _softmax_pv_t(st, vts):
    m = _fold_rows(st, jnp.maximum, jnp.max)
    p = jnp.exp2(st - m).astype(BF16)
    ot, row = None, 0
    for vt in vts:
        part = jnp.dot(vt, p[row:row + vt.shape[1]], preferred_element_type=F32)
        ot = part if ot is None else ot + part
        row += vt.shape[1]
    dv = ot.shape[0] - ONES_ROWS
    return ot[:dv], ot[dv:dv + 1]


def _attend_all_t(units):
    outs = []
    st = _scores_t(units[0][0], units[0][1])
    for i, (_, _, vts) in enumerate(units):
        st_next = _scores_t(units[i + 1][0], units[i + 1][1]) if i + 1 < len(units) else None
        outs.append(_softmax_pv_t(st, vts))
        st = st_next
    return outs


def _mix_kernel(*refs, lam_init, tq, n_kv):
    h_ref, mod_ref, qa_ref, qc_ref = refs[:4]
    kv_refs = [refs[4 + 4 * i:8 + 4 * i] for i in range(n_kv)]
    (ob_ref, yc_ref, wdw_ref, bdw_ref, gconv_ref, gsub_ref, lamc_ref, wout_ref,
     o_ref) = refs[4 + 4 * n_kv:]
    qi = pl.program_id(1)
    n_q = pl.num_programs(1)

    def masked(q, lo, width):
        lane = lax.broadcasted_iota(jnp.int32, q.shape, 1)
        keep = (lane >= lo) & (lane < lo + width)
        return jnp.where(keep, q, 0.0).astype(BF16)

    def with_ones(vt):
        return jnp.concatenate([vt, jnp.ones((ONES_ROWS, vt.shape[1]), BF16)], axis=0)

    qa = qa_ref[0].astype(F32)
    qc = qc_ref[0].astype(F32)
    kas = [kv[0][0] for kv in kv_refs]
    kcs = [kv[2][0] for kv in kv_refs]
    units = []
    for h in range(4):
        g, r = h // 2, h % 2
        vts = [with_ones(kv[1][0, HEAD_DIM * g:HEAD_DIM * (g + 1), :]) for kv in kv_refs]
        units.append((masked(qa[:, 128 * r:128 * (r + 1)], HEAD_DIM * g, HEAD_DIM), kas, vts))
    for h in range(4):
        vts = [with_ones(kv[3][0, HEAD_DIM * h:HEAD_DIM * (h + 1), :]) for kv in kv_refs]
        for c in range(2):
            units.append((masked(qc, HEAD_DIM * h + DIFF_DIM * c, DIFF_DIM), kcs, vts))
    outs = _attend_all_t(units)

    oat = jnp.concatenate([ot / l for ot, l in outs[:4]], axis=0)

    lp = lamc_ref[...]
    lam = (jnp.exp(jnp.sum(lp[0:1] * lp[1:2], keepdims=True))
           - jnp.exp(jnp.sum(lp[2:3] * lp[3:4], keepdims=True)) + lam_init)
    oc = []
    for h in range(4):
        (o0, l0), (o1, l1) = outs[4 + 2 * h], outs[5 + 2 * h]
        o = o0 / l0 - lam * (o1 / l1)
        oc.append(_rms(o, axis=0) * gsub_ref[...] * (1.0 - lam_init))
    oct = jnp.concatenate(oc, axis=0)

    n = yc_ref.shape[1]
    start = pl.multiple_of(qi * tq, 8)
    above = pl.multiple_of(jnp.maximum(start - CONV_PAD, 0), 8)
    below = pl.multiple_of(jnp.minimum(start + tq, n - CONV_PAD), 8)
    win = jnp.concatenate([
        jnp.where(qi > 0, yc_ref[0, pl.ds(above, CONV_PAD), :], 0.0),
        yc_ref[0, pl.ds(start, tq), :],
        jnp.where(qi < n_q - 1, yc_ref[0, pl.ds(below, CONV_PAD), :], 0.0)], axis=0)
    n_win = tq + 2 * CONV_PAD
    acc = None
    for b in range(8):
        shifted = win if b == 0 else pltpu.roll(win, n_win - b, 0)
        for k in range(CONV_K):
            off = CONV_PAD - CONV_K // 2 + k
            if off % 8 != b:
                continue
            term = shifted[off - b:off - b + tq] * wdw_ref[k:k + 1, :]
            acc = term if acc is None else acc + term
    od = _rms(acc + bdw_ref[...]) * gconv_ref[...]
    od = od * jax.nn.sigmoid(od)

    xcat = jnp.concatenate([oat.T.astype(BF16), oct.T.astype(BF16), ob_ref[0], od.astype(BF16)], axis=1)
    mix = jnp.dot(xcat, wout_ref[...], preferred_element_type=F32)
    o_ref[0] = h_ref[0] + mod_ref[0][5:6, :] * mix


def _mix(h, mod, qa, qc, kv_sets, ob, yc, p, w_out, lam_init, tq):
    nb, n, d = h.shape
    tq = min(tq, n)
    per_batch_mod = mod.shape[0] == nb and nb > 1
    mod_map = (lambda b, i: (b, 0, 0)) if per_batch_mod else (lambda b, i: (0, 0, 0))
    tok = lambda w: pl.BlockSpec((1, tq, w), lambda b, i: (b, i, 0))
    per_batch = lambda a: pl.BlockSpec((1,) + a.shape[1:], lambda b, i: (b, 0, 0))
    kv_flat = [a for kv in kv_sets for a in kv]
    return pl.pallas_call(
        functools.partial(_mix_kernel, lam_init=lam_init, tq=tq, n_kv=len(kv_sets)),
        out_shape=jax.ShapeDtypeStruct(h.shape, F32),
        grid=(nb, n // tq),
        in_specs=[
            tok(d),
            pl.BlockSpec((1, N_MOD, d), mod_map),
            tok(256), tok(256),
            *[per_batch(a) for a in kv_flat],
            tok(256),
            per_batch(yc),
            _resident((CONV_K, 256)), _resident((1, 256)), _resident((1, 256)),
            _resident((HEAD_DIM, 1)), _resident((4, DIFF_DIM)),
            _resident((d, d)),
        ],
        out_specs=tok(d),
        compiler_params=_params(2),
        name="mix",
    )(h, mod, qa, qc, *kv_flat, ob, yc,
      p["wdw"], p["bdw"], p["gconv"], p["gsub"], p["lamc"], w_out)


def _rope_tables(n):
    t = np.arange(n)
    row, col = (t // GRID_W).astype(np.float32), (t % GRID_W).astype(np.float32)

    def table(head_dim):
        quarter = head_dim // 4
        inv = (ROPE_THETA ** (-jnp.arange(quarter, dtype=F32) / quarter))
        ang = jnp.concatenate([jnp.asarray(row)[:, None] * inv, jnp.asarray(col)[:, None] * inv], axis=-1)
        cos, sin = jnp.cos(ang), jnp.sin(ang)
        reps = 256 // head_dim
        return (jnp.tile(jnp.concatenate([cos, cos], axis=-1), (1, reps)),
                jnp.tile(jnp.concatenate([-sin, sin], axis=-1), (1, reps)))

    cosa, sina = table(HEAD_DIM)
    cosc, sinc = table(DIFF_DIM)
    return cosa, sina, cosc, sinc


def _layer_params(l, w_ff1_in, w_ff1_out, w_ff2_in, w_ff2_out, w_in, w_out, g_q_a, g_k_a, lam_c,
                  g_sub_c, g_v_b, w_s_b, b_s_b, w_dw_d, b_dw_d, g_conv_d):
    def ff_in(w):
        a = w[:, :D_FF].reshape(D_MODEL, N_FF_CHUNKS, FF_CHUNK)
        b = w[:, D_FF:].reshape(D_MODEL, N_FF_CHUNKS, FF_CHUNK)
        return jnp.concatenate([a, b], axis=-1).transpose(1, 0, 2).astype(BF16)

    wi = w_in[l]
    wi = jnp.concatenate([wi[:, 0:64], wi[:, 128:192], wi[:, 64:128], wi[:, 192:256], wi[:, 256:]], axis=1)
    seg = np.kron(np.eye(256 // HEAD_DIM, dtype=np.float32), np.ones((HEAD_DIM, HEAD_DIM), np.float32))
    return dict(
        ff1_ab=ff_in(w_ff1_in[l]), ff1_o=w_ff1_out[l].astype(BF16),
        ff2_ab=ff_in(w_ff2_in[l]), ff2_o=w_ff2_out[l].astype(BF16),
        w_in=wi.astype(BF16), w_out=w_out[l].astype(BF16),
        gq=jnp.tile(g_q_a[l], 4).reshape(1, 256), gk=jnp.tile(g_k_a[l], 2).reshape(1, 128),
        gv=g_v_b[l].reshape(1, 256),
        ws=w_s_b[l].reshape(4 * CHUNK, CHUNK).astype(BF16),
        bs=jnp.repeat(b_s_b[l].T, HEAD_DIM, axis=1),
        seg=jnp.asarray(seg, BF16),
        wdw=w_dw_d[l], bdw=b_dw_d[l].reshape(1, 256), gconv=g_conv_d[l].reshape(1, 256),
        gsub=g_sub_c[l].reshape(HEAD_DIM, 1), lamc=lam_c[l],
    )


def kernel(x, c, ctx, c_ctx, w_ada, b_ada, g_norm, w_ff1_in, w_ff1_out, w_ff2_in, w_ff2_out, w_in, w_out, g_q_a, g_k_a, lam_c, g_sub_c, g_v_b, w_s_b, b_s_b, w_dw_d, b_dw_d, g_conv_d, g_final):
    nb, n, d = x.shape
    n_ctx = ctx.shape[1]
    depth = w_ada.shape[0]
    tm_ffn = 1024
    tm_proj = 512
    tq = 1024

    rows = -(-(nb + 1) // 8) * 8
    cc = jnp.zeros((rows, d), F32).at[:nb].set(c).at[nb].set(c_ctx)
    mods = _ada(cc, w_ada, b_ada)
    tables = _rope_tables(n)
    tables_ctx = tuple(t[:n_ctx] for t in tables)

    h, hc = x, ctx
    for l in range(depth):
        last = l == depth - 1
        p = _layer_params(l, w_ff1_in, w_ff1_out, w_ff2_in, w_ff2_out, w_in, w_out, g_q_a, g_k_a,
                          lam_c, g_sub_c, g_v_b, w_s_b, b_s_b, w_dw_d, b_dw_d, g_conv_d)
        m = mods[l, :nb].reshape(nb, N_MOD, d)
        mc = mods[l, nb].reshape(1, N_MOD, d)
        lam_init = 0.8 - 0.6 * math.exp(-0.3 * l)

        h = _ffn(h, m, g_norm[l, 0], p["ff1_ab"], p["ff1_o"], 0, tm_ffn)
        hc = _ffn(hc.reshape(1, nb * n_ctx, d), mc, g_norm[l, 0], p["ff1_ab"], p["ff1_o"], 0,
                  tm_ffn).reshape(nb, n_ctx, d)

        qa, qc, ka, vta, kc, vtc, ob, yc = _proj(h, m, g_norm[l, 1], p["w_in"], p, tables, tm_proj, True)
        if last:
            kv_x = _proj_kv(hc, mc, g_norm[l, 1], p["w_in"], p, tm_proj)
        else:
            qa_x, qc_x, ka_x, vta_x, kc_x, vtc_x, ob_x, yc_x = _proj(
                hc, mc, g_norm[l, 1], p["w_in"], p, tables_ctx, tm_proj, False)
            kv_x = (ka_x, vta_x, kc_x, vtc_x)
        h = _mix(h, m, qa, qc, [(ka, vta, kc, vtc), kv_x], ob, yc, p, p["w_out"], lam_init, tq)
        if not last:
            hc = _mix(hc, mc, qa_x, qc_x, [kv_x], ob_x, yc_x, p, p["w_out"], lam_init, tq)

        h = _ffn(h, m, g_norm[l, 2], p["ff2_ab"], p["ff2_o"], 6, tm_ffn,
                 g_final=g_final if last else None)
        if not last:
            hc = _ffn(hc.reshape(1, nb * n_ctx, d), mc, g_norm[l, 2], p["ff2_ab"], p["ff2_o"], 6,
                      tm_ffn).reshape(nb, n_ctx, d)
    return h
```
